```python
import math
import jax
import jax.numpy as jnp
from jax import lax
import numpy as np

D_MODEL = 4096
BATCH = 4
SEQ = 4096
DEPTH = 4
DEC_BATCH = 8
DEC_SEQ = 2048
PAST_LEN = 128

HEAD_DIM = 64
GRID_W = 64
Q_BLOCK = 128
NA_HEADS = 8
NA_ROWS = 8
NA_COLS = 16
NA_QCOLS = 16
NA_KCOLS = 32
DIFF_HEADS = 8
DIFF_D = HEAD_DIM // 2
GQA_HEADS = 8
GQA_KV_HEADS = 2
ROPE_AXIS_DIM = HEAD_DIM // 2
ROPE_THETA = 10000.0
DIL_SLOTS = 8
DIL_GROUPS = ((128, 1), (512, 4), (2048, 16))
N_DIL = 3
T5_BUCKETS = 32
T5_MAX_DIST = 128
T5_HEADS = DIFF_HEADS + N_DIL * DIL_SLOTS
N_BRANCH = 4
BRANCH_W = 8 * HEAD_DIM
GATE_RANK = 256
A_W = NA_HEADS * HEAD_DIM
B_W = DIFF_HEADS * HEAD_DIM
C_Q_W = GQA_HEADS * HEAD_DIM
C_KV_W = GQA_KV_HEADS * HEAD_DIM
D_W = N_DIL * DIL_SLOTS * HEAD_DIM
IN_SIZES = (A_W, A_W, A_W, B_W, B_W, B_W, C_Q_W, C_KV_W, C_KV_W, D_W, D_W, D_W)
D_IN = 3 * A_W + 3 * B_W + C_Q_W + 2 * C_KV_W + 3 * D_W
D_FF = ((8 * D_MODEL + 3 * 256 - 1) // (3 * 256)) * 256
LN_EPS = 1e-5
RMS_EPS = 1e-6
NEG_INF = -1e30
DEEPNORM_ALPHA = (2 * DEPTH) ** 0.25
DEEPNORM_BETA = (8 * DEPTH) ** -0.25

kernel_name = 'hybrid_gated_encoder'


def _layernorm(x, g, b):
    xf = x.astype(jnp.float32)
    mu = jnp.mean(xf, axis=-1, keepdims=True)
    var = jnp.mean(jnp.square(xf - mu), axis=-1, keepdims=True)
    return ((xf - mu) * lax.rsqrt(var + LN_EPS) * g + b).astype(x.dtype)


def _rmsnorm(x, g):
    xf = x.astype(jnp.float32)
    return (xf * lax.rsqrt(jnp.mean(xf * xf, axis=-1, keepdims=True) + RMS_EPS) * g).astype(x.dtype)


def _heads(x, n):
    return x.reshape(x.shape[0], x.shape[1], n, HEAD_DIM)


def _t5_bucket(rel):
    half = T5_BUCKETS // 2
    max_exact = half // 2
    n = jnp.abs(rel)
    nf = jnp.maximum(n, 1).astype(jnp.float32)
    large = max_exact + (jnp.log(nf / max_exact) / math.log(T5_MAX_DIST / max_exact)
                         * (half - max_exact)).astype(jnp.int32)
    large = jnp.minimum(large, half - 1)
    return jnp.where(rel > 0, half, 0) + jnp.where(n < max_exact, n, large)


def _axial_rope_tables(T):
    t = jnp.arange(T, dtype=jnp.int32)
    row = (t // GRID_W).astype(jnp.float32)
    col = (t % GRID_W).astype(jnp.float32)
    freqs = ROPE_THETA ** (-jnp.arange(0, ROPE_AXIS_DIM, 2, dtype=jnp.float32) / ROPE_AXIS_DIM)
    ang = jnp.concatenate([row[:, None] * freqs, col[:, None] * freqs], axis=-1)
    return jnp.cos(ang), jnp.sin(ang)


def _apply_rope(x, cos, sin):
    xf = x.astype(jnp.float32).reshape(*x.shape[:-1], HEAD_DIM // 2, 2)
    x1, x2 = xf[..., 0], xf[..., 1]
    c, s = cos[None, :, None, :], sin[None, :, None, :]
    out = jnp.stack([x1 * c - x2 * s, x1 * s + x2 * c], axis=-1)
    return out.reshape(x.shape).astype(x.dtype)


def _neighbourhood_attention(q, k, v, rpb):
    B, T, H, dh = q.shape
    R = T // GRID_W
    wr = min(NA_ROWS, R)
    n_cb = GRID_W // NA_QCOLS
    qg = q.reshape(B, R, GRID_W, H, dh)
    kg = k.reshape(B, R, GRID_W, H, dh)
    vg = v.reshape(B, R, GRID_W, H, dh)
    qcol = jnp.arange(GRID_W, dtype=jnp.int32).reshape(n_cb, NA_QCOLS)
    c0 = jnp.clip(qcol - NA_COLS // 2, 0, GRID_W - NA_COLS)
    kstart = jnp.clip(jnp.arange(n_cb, dtype=jnp.int32) * NA_QCOLS - NA_COLS // 2, 0, GRID_W - NA_KCOLS)
    kcol = kstart[:, None] + jnp.arange(NA_KCOLS, dtype=jnp.int32)
    col_ok = (kcol[:, None, :] >= c0[..., None]) & (kcol[:, None, :] < c0[..., None] + NA_COLS)
    dcol = jnp.clip(kcol[:, None, :] - qcol[..., None] + NA_COLS - 1, 0, 2 * NA_COLS - 2)
    bias_col = rpb[:, :, dcol]
    scale = dh ** -0.5

    def row_step(r):
        r0 = jnp.clip(r - wr // 2, 0, R - wr)
        kr = lax.dynamic_slice_in_dim(kg, r0, wr, axis=1)[:, :, kcol]
        vr = lax.dynamic_slice_in_dim(vg, r0, wr, axis=1)[:, :, kcol]
        qr = lax.dynamic_index_in_dim(qg, r, axis=1, keepdims=False).reshape(B, n_cb, NA_QCOLS, H, dh)
        s = jnp.einsum('bcqhd,bicjhd->bhcqij', qr, kr, preferred_element_type=jnp.float32) * scale
        drow = r0 + jnp.arange(wr, dtype=jnp.int32) - r + NA_ROWS - 1
        bias = bias_col[:, drow].transpose(0, 2, 3, 1, 4)
        s = jnp.where(col_ok[:, :, None, :], s + bias, NEG_INF)
        p = jax.nn.softmax(s, axis=(-2, -1))
        o = jnp.einsum('bhcqij,bicjhd->bcqhd', p.astype(v.dtype), vr)
        return o.reshape(B, GRID_W, H, dh)

    o = lax.map(row_step, jnp.arange(R, dtype=jnp.int32))
    return o.transpose(1, 0, 2, 3, 4).reshape(B, T, H * dh)


def _diff_attention(q, k, v, t5_tab, lam, lam_init, subln_g):
    B, T, H, _ = q.shape
    nb = T // Q_BLOCK
    qb = (q * DIFF_D ** -0.5).reshape(B, nb, Q_BLOCK, H, 2, DIFF_D).transpose(1, 0, 2, 3, 4, 5)
    k2 = k.reshape(B, T, H, 2, DIFF_D)
    kpos = jnp.arange(T, dtype=jnp.int32)

    def block(args):
        qi, i = args
        qpos = i * Q_BLOCK + jnp.arange(Q_BLOCK, dtype=jnp.int32)
        bias = t5_tab[_t5_bucket(kpos[None, :] - qpos[:, None])].transpose(2, 0, 1)
        s = jnp.einsum('bqhcd,bkhcd->bhcqk', qi, k2, preferred_element_type=jnp.float32) + bias[None, :, None]
        m = jnp.max(s, axis=-1, keepdims=True)
        p = jnp.exp(s - m)
        l = jnp.sum(p, axis=-1)
        o = jnp.einsum('bhcqk,bkhd->bqhcd', p.astype(v.dtype), v, preferred_element_type=jnp.float32)
        o = o / l.transpose(0, 3, 1, 2)[..., None]
        return (o[..., 0, :] - lam * o[..., 1, :]).astype(v.dtype)

    o = lax.map(block, (qb, jnp.arange(nb, dtype=jnp.int32)))
    o = o.transpose(1, 0, 2, 3, 4).reshape(B, T, H, 2 * DIFF_D)
    o = _rmsnorm(o, subln_g) * (1.0 - lam_init)
    return o.reshape(B, T, H * 2 * DIFF_D)


def _gqa_attention(q, k, v):
    B, T, H, dh = q.shape
    kvh = k.shape[2]
    nb = T // Q_BLOCK
    qb = (q * dh ** -0.5).reshape(B, nb, Q_BLOCK, kvh, H // kvh, dh).transpose(1, 0, 2, 3, 4, 5)

    def block(qi):
        s = jnp.einsum('bqngd,bknd->bngqk', qi, k, preferred_element_type=jnp.float32)
        m = jnp.max(s, axis=-1, keepdims=True)
        p = jnp.exp(s - m)
        l = jnp.sum(p, axis=-1)
        o = jnp.einsum('bngqk,bknd->bqngd', p.astype(v.dtype), v, preferred_element_type=jnp.float32)
        return (o / l.transpose(0, 3, 1, 2)[..., None]).astype(v.dtype)

    o = lax.map(block, qb)
    return o.transpose(1, 0, 2, 3, 4, 5).reshape(B, T, H * dh)


def _dilated_group(q, k, v, dil, half, bias_tab):
    B, T, S, dh = q.shape
    L = T // dil
    nqb = -(-L // half)
    Lp = nqb * half

    def sub(x):
        return x.reshape(B, L, dil, S, dh).transpose(0, 2, 1, 3, 4).reshape(B * dil, L, S, dh)

    def windows(x):
        xp = jnp.pad(sub(x), ((0, 0), (half, Lp - L + half), (0, 0), (0, 0)))
        xp = xp.reshape(B * dil, nqb + 2, half, S, dh)
        return jnp.concatenate([xp[:, :-2], xp[:, 1:-1], xp[:, 2:]], axis=2)

    qs = jnp.pad(sub(q), ((0, 0), (0, Lp - L), (0, 0), (0, 0))).reshape(B * dil, nqb, half, S, dh)
    kw, vw = windows(k), windows(v)
    qpos = jnp.arange(Lp, dtype=jnp.int32).reshape(nqb, half)
    kpos = jnp.arange(nqb, dtype=jnp.int32)[:, None] * half - half + jnp.arange(3 * half, dtype=jnp.int32)
    rel = kpos[:, None, :] - qpos[:, :, None]
    valid = (jnp.abs(rel) <= half) & (kpos[:, None, :] >= 0) & (kpos[:, None, :] < L)
    bias = bias_tab[_t5_bucket(rel * dil)].transpose(3, 0, 1, 2)
    s = jnp.einsum('xnqsd,xnksd->xsnqk', qs, kw, preferred_element_type=jnp.float32) * dh ** -0.5
    s = jnp.where(valid, s + bias, NEG_INF)
    m = jnp.max(s, axis=-1, keepdims=True)
    p = jnp.exp(s - m)
    den = jnp.sum(p, axis=-1, keepdims=True)
    o = jnp.einsum('xsnqk,xnksd->xnqsd', (p / den).astype(v.dtype), vw)
    lse = (m + jnp.log(den))[..., 0]
    o = o.reshape(B, dil, Lp, S, dh)[:, :, :L].transpose(0, 2, 1, 3, 4).reshape(B, T, S, dh)
    lse = lse.reshape(B, dil, S, Lp)[..., :L].transpose(0, 3, 1, 2).reshape(B, T, S)
    return o, lse


def _encoder_trunk(x, ln_in_g, ln_in_b, w_in, na_rpb, qk_norm_g, diff_lambda, diff_subln_g,
                   t5_table, w_gate_down, w_gate_up, b_gate, w_branch, w_out, ln1_g, ln1_b,
                   w_ffn_in, w_ffn_out, ln2_g, ln2_b):
    B, T, _ = x.shape
    cos, sin = _axial_rope_tables(T)
    splits = [int(c) for c in np.cumsum(IN_SIZES)[:-1]]
    x = _layernorm(x, ln_in_g, ln_in_b)
    for l in range(DEPTH):
        lam_init = 0.8 - 0.6 * math.exp(-0.3 * l)
        h = jnp.einsum('btd,de->bte', x, w_in[l])
        qa, ka, va, qb, kb, vb, qc, kc, vc, qd, kd, vd = jnp.split(h, splits, axis=-1)
        o_a = _neighbourhood_attention(_heads(qa, NA_HEADS), _heads(ka, NA_HEADS),
                                       _heads(va, NA_HEADS), na_rpb[l])
        lq = diff_lambda[l].astype(jnp.float32)
        lam = jnp.exp(jnp.sum(lq[0] * lq[1])) - jnp.exp(jnp.sum(lq[2] * lq[3])) + lam_init
        o_b = _diff_attention(_heads(qb, DIFF_HEADS), _heads(kb, DIFF_HEADS), _heads(vb, DIFF_HEADS),
                              t5_table[:, :DIFF_HEADS], lam, lam_init, diff_subln_g[l])
        qc_h = _apply_rope(_rmsnorm(_heads(qc, GQA_HEADS), qk_norm_g[l, 0]), cos, sin)
        kc_h = _apply_rope(_rmsnorm(_heads(kc, GQA_KV_HEADS), qk_norm_g[l, 1]), cos, sin)
        o_c = _gqa_attention(qc_h, kc_h, _heads(vc, GQA_KV_HEADS))
        qd_g = qd.reshape(B, T, N_DIL, DIL_SLOTS, HEAD_DIM)
        kd_g = kd.reshape(B, T, N_DIL, DIL_SLOTS, HEAD_DIM)
        vd_g = vd.reshape(B, T, N_DIL, DIL_SLOTS, HEAD_DIM)
        outs, lses = [], []
        for g, (window, dil) in enumerate(DIL_GROUPS):
            lo = DIFF_HEADS + g * DIL_SLOTS
            o_g, lse_g = _dilated_group(qd_g[:, :, g], kd_g[:, :, g], vd_g[:, :, g], dil,
                                        window // (2 * dil), t5_table[:, lo:lo + DIL_SLOTS])
            outs.append(o_g)
            lses.append(lse_g)
        wts = jax.nn.softmax(jnp.stack(lses, axis=-1), axis=-1)
        o_d = jnp.einsum('btsgd,btsg->btsd', jnp.stack(outs, axis=3), wts.astype(x.dtype))
        o_d = o_d.reshape(B, T, DIL_SLOTS * HEAD_DIM)
        gz = jnp.einsum('btd,dr->btr', x, w_gate_down[l])
        gates = jax.nn.sigmoid(jnp.einsum('btr,re->bte', gz, w_gate_up[l]).reshape(B, T, N_BRANCH, D_MODEL)
                               + b_gate[l])
        merged = None
        for n, o in enumerate((o_a, o_b, o_c, o_d)):
            term = gates[:, :, n] * jnp.einsum('btc,cd->btd', o, w_branch[l, n])
            merged = term if merged is None else merged + term
        mix = jnp.einsum('btd,de->bte', merged, w_out[l])
        x = _layernorm(DEEPNORM_ALPHA * x + mix, ln1_g[l], ln1_b[l])
        u = jnp.einsum('btd,df->btf', x, w_ffn_in[l])
        ff = jax.nn.silu(u[..., :D_FF]) * u[..., D_FF:]
        f = jnp.einsum('btf,fd->btd', ff, w_ffn_out[l])
        x = _layernorm(DEEPNORM_ALPHA * x + f, ln2_g[l], ln2_b[l])
    return x


def setup_inputs(seed: int = 0) -> dict:
    key = jax.random.key(seed)
    ks = jax.random.split(key, 21)

    def nrm(k, shape, scale):
        return jax.random.normal(k, shape, jnp.float32) * scale

    return {
        'x_prompt': nrm(ks[0], (BATCH, SEQ, D_MODEL), 1.0),
        'x_sample': nrm(ks[1], (DEC_BATCH, DEC_SEQ, D_MODEL), 1.0),
        'ln_in_g': 1.0 + nrm(ks[2], (D_MODEL,), 0.01),
        'ln_in_b': nrm(ks[3], (D_MODEL,), 0.01),
        'w_in': nrm(ks[4], (DEPTH, D_MODEL, D_IN), D_MODEL ** -0.5),
        'na_rpb': nrm(ks[5], (DEPTH, NA_HEADS, 2 * NA_ROWS - 1, 2 * NA_COLS - 1), 0.1),
        'qk_norm_g': 1.0 + nrm(ks[6], (DEPTH, 2, HEAD_DIM), 0.01),
        'diff_lambda': nrm(ks[7], (DEPTH, 4, DIFF_D), 0.1),
        'diff_subln_g': 1.0 + nrm(ks[8], (DEPTH, 2 * DIFF_D), 0.01),
        't5_table': nrm(ks[9], (T5_BUCKETS, T5_HEADS), 0.1),
        'w_gate_down': nrm(ks[10], (DEPTH, D_MODEL, GATE_RANK), D_MODEL ** -0.5),
        'w_gate_up': nrm(ks[11], (DEPTH, GATE_RANK, N_BRANCH * D_MODEL), GATE_RANK ** -0.5),
        'b_gate': nrm(ks[12], (DEPTH, N_BRANCH, D_MODEL), 0.01),
        'w_branch': nrm(ks[13], (DEPTH, N_BRANCH, BRANCH_W, D_MODEL), BRANCH_W ** -0.5),
        'w_out': nrm(ks[14], (DEPTH, D_MODEL, D_MODEL), D_MODEL ** -0.5 * DEEPNORM_BETA),
        'ln1_g': 1.0 + nrm(ks[15], (DEPTH, D_MODEL), 0.01),
        'ln1_b': nrm(ks[16], (DEPTH, D_MODEL), 0.01),
        'w_ffn_in': nrm(ks[17], (DEPTH, D_MODEL, 2 * D_FF), D_MODEL ** -0.5),
        'w_ffn_out': nrm(ks[18], (DEPTH, D_FF, D_MODEL), D_FF ** -0.5 * DEEPNORM_BETA),
        'ln2_g': 1.0 + nrm(ks[19], (DEPTH, D_MODEL), 0.01),
        'ln2_b': nrm(ks[20], (DEPTH, D_MODEL), 0.01),
    }


def reference(x_prompt, x_sample, ln_in_g, ln_in_b, w_in, na_rpb, qk_norm_g, diff_lambda,
              diff_subln_g, t5_table, w_gate_down, w_gate_up, b_gate, w_branch, w_out, ln1_g,
              ln1_b, w_ffn_in, w_ffn_out, ln2_g, ln2_b):
    y_prompt = _encoder_trunk(x_prompt, ln_in_g, ln_in_b, w_in, na_rpb, qk_norm_g, diff_lambda,
                              diff_subln_g, t5_table, w_gate_down, w_gate_up, b_gate, w_branch,
                              w_out, ln1_g, ln1_b, w_ffn_in, w_ffn_out, ln2_g, ln2_b)
    y_sample = _encoder_trunk(x_sample, ln_in_g, ln_in_b, w_in, na_rpb, qk_norm_g, diff_lambda,
                              diff_subln_g, t5_table, w_gate_down, w_gate_up, b_gate, w_branch,
                              w_out, ln1_g, ln1_b, w_ffn_in, w_ffn_out, ln2_g, ln2_b)
    return (y_prompt, y_sample)
```

```python
import functools
import math

import numpy as np
import jax
import jax.numpy as jnp
from jax import lax
from jax.experimental import pallas as pl
from jax.experimental.pallas import tpu as pltpu

F32 = jnp.float32
BF16 = jnp.bfloat16

HEAD_DIM = 64
LANES = 128
GRID_W = 64
NA_ROWS = 8
NA_COLS = 16
DIFF_HEADS = 8
DIFF_D = HEAD_DIM // 2
GQA_HEADS = 8
GQA_KV_HEADS = 2
ROPE_AXIS_DIM = HEAD_DIM // 2
ROPE_THETA = 10000.0
DIL_SLOTS = 8
DIL_GROUPS = ((128, 1), (512, 4), (2048, 16))
T5_BUCKETS = 32
T5_MAX_DIST = 128
N_BRANCH = 4
BRANCH_W = 8 * HEAD_DIM
GATE_RANK = 256
LN_EPS = 1e-5
RMS_EPS = 1e-6
NEG_INF = -1e30

COL_QA, COL_KA, COL_VA = 0, 512, 1024
COL_QB, COL_KB, COL_VB = 1536, 2048, 2560
COL_QC = 3072
COL_QD, COL_KD, COL_VD = 3584, 5120, 6656
COL_KC, COL_VC = 8192, 8320
COL_GZ = 8448
H_COLS = 8704
GQA_HEAD_ORDER = (0, 4, 1, 5, 2, 6, 3, 7)

VMEM_LIMIT = 56 * 1024 * 1024


def _params(n_axes, vmem=VMEM_LIMIT):
    return pltpu.CompilerParams(dimension_semantics=("arbitrary",) * n_axes, vmem_limit_bytes=vmem)


def _pick(n, pref):
    t = min(n, pref)
    while n % t:
        t //= 2
    return t


def _ln_rows(y, g, b):
    mu = jnp.mean(y, axis=-1, keepdims=True)
    yc = y - mu
    var = jnp.mean(yc * yc, axis=-1, keepdims=True)
    return yc * lax.rsqrt(var + LN_EPS) * g + b


def _ln_in_kernel(x_ref, g_ref, b_ref, xo_ref, xb_ref):
    r = _ln_rows(x_ref[...], g_ref[...], b_ref[...])
    xo_ref[...] = r
    xb_ref[...] = r.astype(BF16)


def _ln_in(x, g, b):
    M, D = x.shape
    tm = _pick(M, 256)
    row = pl.BlockSpec((tm, D), lambda i: (i, 0))
    vec = pl.BlockSpec((1, D), lambda i: (0, 0))
    return pl.pallas_call(
        _ln_in_kernel,
        grid=(M // tm,),
        in_specs=[row, vec, vec],
        out_specs=[row, row],
        out_shape=[jax.ShapeDtypeStruct((M, D), F32), jax.ShapeDtypeStruct((M, D), BF16)],
        compiler_params=_params(1),
    )(x, g.reshape(1, D), b.reshape(1, D))


def _mm_kernel(x_ref, w_ref, o_ref):
    o_ref[...] = jnp.dot(x_ref[...], w_ref[...], preferred_element_type=F32).astype(o_ref.dtype)


def _matmul(x, w, tm, tn):
    M, K = x.shape
    N = w.shape[1]
    tm, tn = _pick(M, tm), _pick(N, tn)
    return pl.pallas_call(
        _mm_kernel,
        grid=(M // tm, N // tn),
        in_specs=[pl.BlockSpec((tm, K), lambda i, j: (i, 0)), pl.BlockSpec((K, tn), lambda i, j: (0, j))],
        out_specs=pl.BlockSpec((tm, tn), lambda i, j: (i, j)),
        out_shape=jax.ShapeDtypeStruct((M, N), BF16),
        compiler_params=_params(2),
    )(x, w)


def _swiglu_kernel(x_ref, wa_ref, wb_ref, o_ref):
    x = x_ref[...]
    a = jnp.dot(x, wa_ref[...], preferred_element_type=F32)
    b = jnp.dot(x, wb_ref[...], preferred_element_type=F32)
    o_ref[...] = (a * (1.0 / (1.0 + jnp.exp(-a))) * b).astype(o_ref.dtype)


def _swiglu(x, w, tm, tn):
    M, K = x.shape
    F = w.shape[1] // 2
    tm, tn = _pick(M, tm), _pick(F, tn)
    nf = F // tn
    return pl.pallas_call(
        _swiglu_kernel,
        grid=(M // tm, nf),
        in_specs=[pl.BlockSpec((tm, K), lambda i, j: (i, 0)),
                  pl.BlockSpec((K, tn), lambda i, j: (0, j)),
                  pl.BlockSpec((K, tn), lambda i, j: (0, j + nf))],
        out_specs=pl.BlockSpec((tm, tn), lambda i, j: (i, j)),
        out_shape=jax.ShapeDtypeStruct((M, F), BF16),
        compiler_params=_params(2),
    )(x, w, w)


def _mm_res_ln_kernel(a_ref, w_ref, xres_ref, g_ref, b_ref, xo_ref, xb_ref, *, nk, alpha):
    k = pl.program_id(1)
    part = jnp.dot(a_ref[...], w_ref[...], preferred_element_type=F32)

    @pl.when(k == 0)
    def _():
        xo_ref[...] = part

    @pl.when(k > 0)
    def _():
        xo_ref[...] += part

    @pl.when(k == nk - 1)
    def _():
        r = _ln_rows(alpha * xres_ref[...] + xo_ref[...], g_ref[...], b_ref[...])
        xo_ref[...] = r
        xb_ref[...] = r.astype(BF16)


def _mm_res_ln(a, w, xres, g, b, alpha, tm, tk):
    M, K = a.shape
    D = w.shape[1]
    tm, tk = _pick(M, tm), _pick(K, tk)
    nk = K // tk
    row = pl.BlockSpec((tm, D), lambda i, k: (i, 0))
    vec = pl.BlockSpec((1, D), lambda i, k: (0, 0))
    return pl.pallas_call(
        functools.partial(_mm_res_ln_kernel, nk=nk, alpha=alpha),
        grid=(M // tm, nk),
        in_specs=[pl.BlockSpec((tm, tk), lambda i, k: (i, k)),
                  pl.BlockSpec((tk, D), lambda i, k: (k, 0)),
                  row, vec, vec],
        out_specs=[row, row],
        out_shape=[jax.ShapeDtypeStruct((M, D), F32), jax.ShapeDtypeStruct((M, D), BF16)],
        compiler_params=_params(2),
    )(a, w, xres, g.reshape(1, D), b.reshape(1, D))


def _gate_merge_kernel(gz_ref, oa_ref, ob_ref, oc_ref, od_ref, wup_ref, bg_ref, wbr_ref, o_ref):
    gz = gz_ref[...]
    acc = None
    for n, o_n in enumerate((oa_ref, ob_ref, oc_ref, od_ref)):
        z = jnp.dot(gz, wup_ref[n], preferred_element_type=F32) + bg_ref[n:n + 1, :]
        t = jnp.dot(o_n[...], wbr_ref[n], preferred_element_type=F32)
        term = t * (1.0 / (1.0 + jnp.exp(-z)))
        acc = term if acc is None else acc + term
    o_ref[...] = acc.astype(o_ref.dtype)


def _gate_merge(h, o_a, o_b, o_c, o_d, wup, bg, wbr, tm, tn):
    M = h.shape[0]
    D = wup.shape[2]
    tm, tn = _pick(M, tm), _pick(D, tn)
    o_spec = pl.BlockSpec((tm, BRANCH_W), lambda i, j: (i, 0))
    return pl.pallas_call(
        _gate_merge_kernel,
        grid=(M // tm, D // tn),
        in_specs=[pl.BlockSpec((tm, GATE_RANK), lambda i, j: (i, COL_GZ // GATE_RANK)),
                  o_spec, o_spec, o_spec, o_spec,
                  pl.BlockSpec((N_BRANCH, GATE_RANK, tn), lambda i, j: (0, 0, j)),
                  pl.BlockSpec((N_BRANCH, tn), lambda i, j: (0, j)),
                  pl.BlockSpec((N_BRANCH, BRANCH_W, tn), lambda i, j: (0, 0, j))],
        out_specs=pl.BlockSpec((tm, tn), lambda i, j: (i, j)),
        out_shape=jax.ShapeDtypeStruct((M, D), BF16),
        compiler_params=_params(2),
    )(h, o_a, o_b, o_c, o_d, wup, bg, wbr)


def _lane():
    return lax.broadcasted_iota(jnp.int32, (1, LANES), 1)


def _dot_nt(a, b):
    return lax.dot_general(a, b, (((1,), (1,)), ((), ())), preferred_element_type=F32)


def _split_heads(x):
    lo = (_lane() < HEAD_DIM).astype(F32)
    return jnp.concatenate([(x * lo).astype(BF16), (x * (1.0 - lo)).astype(BF16)], axis=0)


def _merge_heads(o, rows):
    return jnp.where(_lane() < HEAD_DIM, o[:rows], o[rows:])


def _group_mean(x):
    r = lax.broadcasted_iota(jnp.int32, (LANES, LANES), 0) >> 6
    c = lax.broadcasted_iota(jnp.int32, (LANES, LANES), 1) >> 6
    p = jnp.where(r == c, 1.0 / HEAD_DIM, 0.0).astype(BF16)
    hi = x.astype(BF16)
    lo = (x - hi.astype(F32)).astype(BF16)
    return jnp.dot(hi, p, preferred_element_type=F32) + jnp.dot(lo, p, preferred_element_type=F32)


def _rms_heads(x, g):
    return x * lax.rsqrt(_group_mean(x * x) + RMS_EPS) * g


def _flash(qs, k_ref, v_ref, n_chunks, tk, bias_fn=None):
    rows = qs.shape[0]

    def body(kc, carry):
        m, l, acc = carry
        ks = pl.multiple_of(kc * tk, tk)
        s = _dot_nt(qs, k_ref[pl.ds(ks, tk), :])
        if bias_fn is not None:
            s = s + bias_fn(kc)
        m_new = jnp.maximum(m, jnp.max(s, axis=-1, keepdims=True))
        a = jnp.exp(m - m_new)
        p = jnp.exp(s - m_new)
        l = a * l + jnp.sum(p, axis=-1, keepdims=True)
        acc = a * acc + jnp.dot(p.astype(BF16), v_ref[pl.ds(ks, tk), :], preferred_element_type=F32)
        return m_new, l, acc

    init = (jnp.full((rows, 1), NEG_INF, F32), jnp.zeros((rows, 1), F32), jnp.zeros((rows, LANES), F32))
    _, l, acc = lax.fori_loop(0, n_chunks, body, init)
    return acc / l


def _t5_bucket(rel):
    half = T5_BUCKETS // 2
    max_exact = half // 2
    n = jnp.abs(rel)
    nf = jnp.maximum(n, 1).astype(jnp.float32)
    large = max_exact + (jnp.log(nf / max_exact) / math.log(T5_MAX_DIST / max_exact)
                         * (half - max_exact)).astype(jnp.int32)
    large = jnp.minimum(large, half - 1)
    return jnp.where(rel > 0, half, 0) + jnp.where(n < max_exact, n, large)


def _na_bias_table(rpb):
    qcol = jnp.arange(GRID_W, dtype=jnp.int32)
    kcol = jnp.arange(GRID_W, dtype=jnp.int32)
    c0 = jnp.clip(qcol - NA_COLS // 2, 0, GRID_W - NA_COLS)
    col_ok = (kcol[None, :] >= c0[:, None]) & (kcol[None, :] < c0[:, None] + NA_COLS)
    dcol = jnp.clip(kcol[None, :] - qcol[:, None] + NA_COLS - 1, 0, 2 * NA_COLS - 2)
    rpb_col = rpb[:, :, dcol]
    drow = jnp.arange(NA_ROWS)[None, :] - jnp.arange(NA_ROWS)[:, None] + NA_ROWS - 1
    t = rpb_col[:, drow]
    t = jnp.where(col_ok[None, None, None], t, NEG_INF)
    t = t.transpose(1, 0, 3, 2, 4)
    return t.reshape(NA_ROWS, t.shape[1], GRID_W, NA_ROWS * GRID_W).astype(F32)


def _na_kernel(q_ref, k_ref, v_ref, bias_ref, o_ref, *, R):
    r = pl.program_id(1)
    r0 = jnp.clip(r - NA_ROWS // 2, 0, R - NA_ROWS)
    delta = r - r0
    ks = pl.multiple_of(r0 * GRID_W, GRID_W)
    nk = NA_ROWS * GRID_W
    for c in range(BRANCH_W // LANES):
        cols = slice(c * LANES, (c + 1) * LANES)
        qs = _split_heads(q_ref[:, cols].astype(F32) * HEAD_DIM ** -0.5)
        s = _dot_nt(qs, k_ref[pl.ds(ks, nk), cols])
        s = s + jnp.concatenate([bias_ref[delta, 2 * c], bias_ref[delta, 2 * c + 1]], axis=0)
        m = jnp.max(s, axis=-1, keepdims=True)
        p = jnp.exp(s - m)
        l = jnp.sum(p, axis=-1, keepdims=True)
        o = jnp.dot(p.astype(BF16), v_ref[pl.ds(ks, nk), cols], preferred_element_type=F32) / l
        o_ref[:, cols] = _merge_heads(o, GRID_W).astype(o_ref.dtype)


def _na_attention(h, bias, B, T):
    R = T // GRID_W
    assert R >= NA_ROWS
    M = h.shape[0]
    return pl.pallas_call(
        functools.partial(_na_kernel, R=R),
        grid=(B, R),
        in_specs=[pl.BlockSpec((GRID_W, BRANCH_W), lambda b, r: (b * R + r, COL_QA // BRANCH_W)),
                  pl.BlockSpec((T, BRANCH_W), lambda b, r: (b, COL_KA // BRANCH_W)),
                  pl.BlockSpec((T, BRANCH_W), lambda b, r: (b, COL_VA // BRANCH_W)),
                  pl.BlockSpec(bias.shape, lambda b, r: (0, 0, 0, 0))],
        out_specs=pl.BlockSpec((GRID_W, BRANCH_W), lambda b, r: (b * R + r, 0)),
        out_shape=jax.ShapeDtypeStruct((M, BRANCH_W), BF16),
        compiler_params=_params(2),
    )(h, h, h, bias)


DIFF_TQ, DIFF_TK = 256, 512
DIFF_D_LO = -(DIFF_TK + DIFF_TQ)
DIFF_D_HI = 2 * DIFF_TQ
assert DIFF_TQ >= T5_MAX_DIST and DIFF_TK % DIFF_TQ == 0


def _diff_bias_tiles(t5_tab):
    d = jnp.arange(DIFF_D_LO, DIFF_D_HI + 1, DIFF_TQ, dtype=jnp.int32)
    rel = (d[:, None, None] + jnp.arange(DIFF_TK, dtype=jnp.int32)[None, None, :]
           - jnp.arange(DIFF_TQ, dtype=jnp.int32)[None, :, None])
    return t5_tab[_t5_bucket(rel)].transpose(3, 0, 1, 2).astype(F32)


def _diff_kernel(lam_ref, q_ref, k_ref, v_ref, tiles_ref, g_ref, o_ref, *, n_chunks, out_scale):
    qi = pl.program_id(2)
    tq, tk = DIFF_TQ, DIFF_TK
    lane = _lane()
    qf = q_ref[...].astype(F32) * DIFF_D ** -0.5
    qs = jnp.concatenate([(qf * ((lane >> 5) == v).astype(F32)).astype(BF16) for v in range(4)], axis=0)

    def bias_fn(kc):
        d = jnp.clip(kc * tk - qi * tq, DIFF_D_LO, DIFF_D_HI)
        j = (d - DIFF_D_LO) // tq
        b0 = tiles_ref[0, j]
        b1 = tiles_ref[1, j]
        return jnp.concatenate([b0, b0, b1, b1], axis=0)

    o = _flash(qs, k_ref, v_ref, n_chunks, tk, bias_fn)
    lam = lam_ref[0]
    x = jnp.where(lane < HEAD_DIM, o[:tq] - lam * o[tq:2 * tq], o[2 * tq:3 * tq] - lam * o[3 * tq:])
    o_ref[...] = (_rms_heads(x, g_ref[...]) * out_scale).astype(o_ref.dtype)


def _diff_attention(h, tiles, lam, g, lam_init, B, T):
    M = h.shape[0]
    tq, tk = DIFF_TQ, DIFF_TK
    assert T % tk == 0
    nq = T // tq
    qb, kb, vb = COL_QB // LANES, COL_KB // LANES, COL_VB // LANES
    g2 = jnp.tile(g.astype(F32), 2).reshape(1, LANES)
    return pl.pallas_call(
        functools.partial(_diff_kernel, n_chunks=T // tk, out_scale=1.0 - lam_init),
        grid=(B, DIFF_HEADS // 2, nq),
        in_specs=[pl.BlockSpec(memory_space=pltpu.SMEM),
                  pl.BlockSpec((tq, LANES), lambda b, hp, qi: (b * nq + qi, qb + hp)),
                  pl.BlockSpec((T, LANES), lambda b, hp, qi: (b, kb + hp)),
                  pl.BlockSpec((T, LANES), lambda b, hp, qi: (b, vb + hp)),
                  pl.BlockSpec((2,) + tiles.shape[1:], lambda b, hp, qi: (hp, 0, 0, 0)),
                  pl.BlockSpec((1, LANES), lambda b, hp, qi: (0, 0))],
        out_specs=pl.BlockSpec((tq, LANES), lambda b, hp, qi: (b * nq + qi, hp)),
        out_shape=jax.ShapeDtypeStruct((M, BRANCH_W), BF16),
        compiler_params=_params(3),
    )(lam.reshape(1).astype(F32), h, h, h, tiles, g2)


GQA_TQ, GQA_TK = 128, 512


def _rope_tables(T):
    t = jnp.arange(T, dtype=jnp.int32)
    row = (t // GRID_W).astype(jnp.float32)
    col = (t % GRID_W).astype(jnp.float32)
    freqs = ROPE_THETA ** (-jnp.arange(0, ROPE_AXIS_DIM, 2, dtype=jnp.float32) / ROPE_AXIS_DIM)
    ang = jnp.concatenate([row[:, None] * freqs, col[:, None] * freqs], axis=-1)
    cos, sin = jnp.cos(ang), jnp.sin(ang)
    cos_t = jnp.tile(cos, (1, 4))
    sin_t = jnp.tile(jnp.concatenate([-sin, sin], axis=-1), (1, 2))
    return cos_t.astype(F32), sin_t.astype(F32)


def _rope(x, cos, sin):
    first = (_lane() & (HEAD_DIM - 1)) < HEAD_DIM // 2
    partner = jnp.where(first, pltpu.roll(x, LANES - HEAD_DIM // 2, axis=1), pltpu.roll(x, HEAD_DIM // 2, axis=1))
    return x * cos + partner * sin


def _gqa_kernel(q_ref, k_ref, v_ref, cq_ref, sq_ref, ck_ref, sk_ref, gq_ref, gk_ref, o_ref, kproc_ref, *, n_chunks):
    qi = pl.program_id(1)
    tq = GQA_TQ

    @pl.when(qi == 0)
    def _():
        kf = _rms_heads(k_ref[...].astype(F32), gk_ref[...])
        kproc_ref[...] = _rope(kf, ck_ref[...], sk_ref[...]).astype(BF16)

    cq, sq, gq = cq_ref[...], sq_ref[...], gq_ref[...]
    parts = []
    for c in range(BRANCH_W // LANES):
        qf = _rms_heads(q_ref[:, c * LANES:(c + 1) * LANES].astype(F32), gq)
        parts.append(_split_heads(_rope(qf, cq, sq) * HEAD_DIM ** -0.5))
    o = _flash(jnp.concatenate(parts, axis=0), kproc_ref, v_ref, n_chunks, GQA_TK)
    for c in range(BRANCH_W // LANES):
        o_ref[:, c * LANES:(c + 1) * LANES] = _merge_heads(o[2 * c * tq:(2 * c + 2) * tq], tq).astype(o_ref.dtype)


def _gqa_attention(h, cos_t, sin_t, gq, gk, B, T):
    M = h.shape[0]
    tq, tk = GQA_TQ, GQA_TK
    assert T % tk == 0
    nq = T // tq
    vec = pl.BlockSpec((1, LANES), lambda b, qi: (0, 0))
    qtab = pl.BlockSpec((tq, LANES), lambda b, qi: (qi, 0))
    ktab = pl.BlockSpec((T, LANES), lambda b, qi: (0, 0))
    return pl.pallas_call(
        functools.partial(_gqa_kernel, n_chunks=T // tk),
        grid=(B, nq),
        in_specs=[pl.BlockSpec((tq, BRANCH_W), lambda b, qi: (b * nq + qi, COL_QC // BRANCH_W)),
                  pl.BlockSpec((T, LANES), lambda b, qi: (b, COL_KC // LANES)),
                  pl.BlockSpec((T, LANES), lambda b, qi: (b, COL_VC // LANES)),
                  qtab, qtab, ktab, ktab, vec, vec],
        out_specs=pl.BlockSpec((tq, BRANCH_W), lambda b, qi: (b * nq + qi, 0)),
        out_shape=jax.ShapeDtypeStruct((M, BRANCH_W), BF16),
        scratch_shapes=[pltpu.VMEM((T, LANES), BF16)],
        compiler_params=_params(2),
    )(h, h, h, cos_t, sin_t, cos_t, sin_t, gq, gk)


DIL_TQ = 128
DIL_HALF = 64
assert all(w // (2 * d) == DIL_HALF for w, d in DIL_GROUPS)


def _dil_bias_tiles(t5_tab, g, L):
    dil = DIL_GROUPS[g][1]
    W = min(2 * DIL_TQ, L)
    offs = jnp.arange(0, -(W - DIL_TQ) - 1, -DIL_HALF, dtype=jnp.int32)
    rel = (offs[:, None, None] + jnp.arange(W, dtype=jnp.int32)[None, None, :]
           - jnp.arange(DIL_TQ, dtype=jnp.int32)[None, :, None])
    lo = DIFF_HEADS + g * DIL_SLOTS
    bias = t5_tab[:, lo:lo + DIL_SLOTS][_t5_bucket(rel * dil)]
    bias = jnp.where((jnp.abs(rel) <= DIL_HALF)[..., None], bias, NEG_INF)
    return bias.transpose(0, 3, 1, 2).astype(F32)


def _dil_kernel(q_ref, k_ref, v_ref, tiles_ref, o_ref, lse_ref, *, L, W):
    n = pl.program_id(2)
    tq = DIL_TQ
    ws = jnp.clip(n * tq - DIL_HALF, 0, L - W)
    var = (n * tq - ws) // DIL_HALF
    ws = pl.multiple_of(ws, DIL_HALF)
    for c in range(BRANCH_W // LANES):
        cols = slice(c * LANES, (c + 1) * LANES)
        qs = _split_heads(q_ref[:, cols].astype(F32) * HEAD_DIM ** -0.5)
        s = _dot_nt(qs, k_ref[pl.ds(ws, W), cols])
        s = s + jnp.concatenate([tiles_ref[var, 2 * c], tiles_ref[var, 2 * c + 1]], axis=0)
        m = jnp.max(s, axis=-1, keepdims=True)
        p = jnp.exp(s - m)
        l = jnp.sum(p, axis=-1, keepdims=True)
        o = jnp.dot(p.astype(BF16), v_ref[pl.ds(ws, W), cols], preferred_element_type=F32) / l
        lse = jnp.broadcast_to(m + jnp.log(l), (2 * tq, LANES))
        o_ref[:, cols] = _merge_heads(o, tq).astype(o_ref.dtype)
        lse_ref[:, cols] = _merge_heads(lse, tq)


def _dil_attention(h, tiles, g, B, T):
    dil = DIL_GROUPS[g][1]
    M = h.shape[0]
    L = T // dil
    tq = DIL_TQ
    W = min(2 * tq, L)
    assert L % tq == 0
    nq = L // tq
    hv = h.reshape(M // dil, dil * H_COLS)
    nb = H_COLS // BRANCH_W
    qd, kd, vd = COL_QD // BRANCH_W + g, COL_KD // BRANCH_W + g, COL_VD // BRANCH_W + g
    out_spec = pl.BlockSpec((tq, BRANCH_W), lambda b, rho, n: (b * nq + n, rho))
    o, lse = pl.pallas_call(
        functools.partial(_dil_kernel, L=L, W=W),
        grid=(B, dil, nq),
        in_specs=[pl.BlockSpec((tq, BRANCH_W), lambda b, rho, n: (b * nq + n, rho * nb + qd)),
                  pl.BlockSpec((L, BRANCH_W), lambda b, rho, n: (b, rho * nb + kd)),
                  pl.BlockSpec((L, BRANCH_W), lambda b, rho, n: (b, rho * nb + vd)),
                  pl.BlockSpec(tiles.shape, lambda b, rho, n: (0, 0, 0, 0))],
        out_specs=[out_spec, out_spec],
        out_shape=[jax.ShapeDtypeStruct((M // dil, dil * BRANCH_W), BF16),
                   jax.ShapeDtypeStruct((M // dil, dil * BRANCH_W), F32)],
        compiler_params=_params(3),
    )(hv, hv, hv, tiles)
    return o.reshape(M, BRANCH_W), lse.reshape(M, BRANCH_W)


def _dil_merge_kernel(o0_ref, o1_ref, o2_ref, l0_ref, l1_ref, l2_ref, o_ref):
    ls = [l0_ref[...], l1_ref[...], l2_ref[...]]
    mx = jnp.maximum(jnp.maximum(ls[0], ls[1]), ls[2])
    es = [jnp.exp(l - mx) for l in ls]
    den = es[0] + es[1] + es[2]
    num = (es[0] * o0_ref[...].astype(F32) + es[1] * o1_ref[...].astype(F32) + es[2] * o2_ref[...].astype(F32))
    o_ref[...] = (num / den).astype(o_ref.dtype)


def _dil_merge(outs, lses):
    M = outs[0].shape[0]
    tm = _pick(M, 1024)
    spec = pl.BlockSpec((tm, BRANCH_W), lambda i: (i, 0))
    return pl.pallas_call(
        _dil_merge_kernel,
        grid=(M // tm,),
        in_specs=[spec] * 6,
        out_specs=spec,
        out_shape=jax.ShapeDtypeStruct((M, BRANCH_W), BF16),
        compiler_params=_params(1),
    )(*outs, *lses)


def _deinterleave(w, n_heads):
    d = w.shape[0]
    return w.reshape(d, n_heads, HEAD_DIM // 2, 2).transpose(0, 1, 3, 2).reshape(d, n_heads * HEAD_DIM)


def _prep_layer(l, w_in, w_gate_down, w_gate_up, w_branch, w_out, w_ffn_in, w_ffn_out, ff_pad):
    d = w_in.shape[1]
    wi = w_in[l]
    qc = _deinterleave(wi[:, 3072:3584], GQA_HEADS).reshape(d, GQA_HEADS, HEAD_DIM)
    qc = qc[:, jnp.array(GQA_HEAD_ORDER)].reshape(d, GQA_HEADS * HEAD_DIM)
    kc = _deinterleave(wi[:, 3584:3712], GQA_KV_HEADS)
    w_cat = jnp.concatenate([wi[:, :3072], qc, wi[:, 3840:], kc, wi[:, 3712:3840], w_gate_down[l]], axis=1)
    assert w_cat.shape[1] == H_COLS
    wup = w_gate_up[l].reshape(GATE_RANK, N_BRANCH, d).transpose(1, 0, 2)
    wbr = w_branch[l]
    wbr_c = wbr[2].reshape(GQA_HEADS, HEAD_DIM, d)[jnp.array(GQA_HEAD_ORDER)].reshape(BRANCH_W, d)
    wbr = jnp.stack([wbr[0], wbr[1], wbr_c, wbr[3]])
    ff = w_ffn_out.shape[1]
    wfi = w_ffn_in[l]
    pad = ff_pad - ff
    wfi = jnp.concatenate([jnp.pad(wfi[:, :ff], ((0, 0), (0, pad))), jnp.pad(wfi[:, ff:], ((0, 0), (0, pad)))], axis=1)
    wfo = jnp.pad(w_ffn_out[l], ((0, pad), (0, 0)))
    return dict(w_cat=w_cat.astype(BF16), wup=wup.astype(BF16), wbr=wbr.astype(BF16), w_out=w_out[l].astype(BF16),
                wfi=wfi.astype(BF16), wfo=wfo.astype(BF16))


def _trunk(x, consts, layers, p):
    B, T, D = x.shape
    M = B * T
    depth = len(layers)
    alpha = (2 * depth) ** 0.25
    cos_t, sin_t = _rope_tables(T)
    dil_tiles = [_dil_bias_tiles(p['t5_table'], g, T // dil) for g, (_, dil) in enumerate(DIL_GROUPS)]
    xf, xb = _ln_in(x.reshape(M, D), p['ln_in_g'], p['ln_in_b'])
    for l, lw in enumerate(layers):
        lam_init = 0.8 - 0.6 * math.exp(-0.3 * l)
        h = _matmul(xb, lw['w_cat'], 1024, 512)
        o_a = _na_attention(h, consts['na_bias'][l], B, T)
        lq = p['diff_lambda'][l].astype(F32)
        lam = jnp.exp(jnp.sum(lq[0] * lq[1])) - jnp.exp(jnp.sum(lq[2] * lq[3])) + lam_init
        o_b = _diff_attention(h, consts['diff_tiles'], lam, p['diff_subln_g'][l], lam_init, B, T)
        gq = jnp.tile(jnp.concatenate([p['qk_norm_g'][l, 0, 0::2], p['qk_norm_g'][l, 0, 1::2]]), 2).reshape(1, LANES)
        gk = jnp.tile(jnp.concatenate([p['qk_norm_g'][l, 1, 0::2], p['qk_norm_g'][l, 1, 1::2]]), 2).reshape(1, LANES)
        o_c = _gqa_attention(h, cos_t, sin_t, gq.astype(F32), gk.astype(F32), B, T)
        outs, lses = [], []
        for g in range(len(DIL_GROUPS)):
            o_g, lse_g = _dil_attention(h, dil_tiles[g], g, B, T)
            outs.append(o_g)
            lses.append(lse_g)
        o_d = _dil_merge(outs, lses)
        merged = _gate_merge(h, o_a, o_b, o_c, o_d, lw['wup'], p['b_gate'][l].astype(F32), lw['wbr'], 512, 1024)
        xf, xb = _mm_res_ln(merged, lw['w_out'], xf, p['ln1_g'][l], p['ln1_b'][l], alpha, 256, 512)
        ff = _swiglu(xb, lw['wfi'], 1024, 512)
        xf, xb = _mm_res_ln(ff, lw['wfo'], xf, p['ln2_g'][l], p['ln2_b'][l], alpha, 256, 1024)
    return xf.reshape(B, T, D)


def kernel(x_prompt, x_sample, ln_in_g, ln_in_b, w_in, na_rpb, qk_norm_g, diff_lambda, diff_subln_g, t5_table,
           w_gate_down, w_gate_up, b_gate, w_branch, w_out, ln1_g, ln1_b, w_ffn_in, w_ffn_out, ln2_g, ln2_b):
    depth = w_in.shape[0]
    ff = w_ffn_out.shape[1]
    ff_pad = -(-ff // 1024) * 1024
    layers = [_prep_layer(l, w_in, w_gate_down, w_gate_up, w_branch, w_out, w_ffn_in, w_ffn_out, ff_pad)
              for l in range(depth)]
    consts = dict(na_bias=[_na_bias_table(na_rpb[l]) for l in range(depth)],
                  diff_tiles=_diff_bias_tiles(t5_table[:, :DIFF_HEADS]))
    p = dict(ln_in_g=ln_in_g, ln_in_b=ln_in_b, qk_norm_g=qk_norm_g, diff_lambda=diff_lambda,
             diff_subln_g=diff_subln_g, t5_table=t5_table, b_gate=b_gate, ln1_g=ln1_g, ln1_b=ln1_b,
             ln2_g=ln2_g, ln2_b=ln2_b)
    return (_trunk(x_prompt, consts, layers, p), _trunk(x_sample, consts, layers, p))
```

```python
import functools
import math

import jax
import jax.numpy as jnp
from jax import lax
from jax.experimental import pallas as pl
from jax.experimental.pallas import tpu as pltpu

F32 = jnp.float32
BF16 = jnp.bfloat16

HEAD_DIM = 64
LANES = 128
GRID_W = 64
NA_ROWS = 8
NA_COLS = 16
DIFF_HEADS = 8
DIFF_D = HEAD_DIM // 2
GQA_HEADS = 8
GQA_KV_HEADS = 2
ROPE_AXIS_DIM = HEAD_DIM // 2
ROPE_THETA = 10000.0
DIL_SLOTS = 8
DIL_GROUPS = ((128, 1), (512, 4), (2048, 16))
T5_BUCKETS = 32
T5_MAX_DIST = 128
N_BRANCH = 4
BRANCH_W = 8 * HEAD_DIM
GATE_RANK = 256
LN_EPS = 1e-5
RMS_EPS = 1e-6
NEG_INF = -1e30

COL_QA, COL_KA, COL_VA = 0, 512, 1024
COL_QB, COL_KB, COL_VB = 1536, 2048, 2560
COL_QC = 3072
COL_KC, COL_VC = 3584, 3712
COL_GZ = 3840
COL_D0 = 4096
H_MAIN_COLS = COL_D0 + 3 * BRANCH_W
GQA_HEAD_ORDER = (0, 4, 1, 5, 2, 6, 3, 7)

VMEM_LIMIT = 56 * 1024 * 1024


def _params(n_axes, vmem=VMEM_LIMIT):
    return pltpu.CompilerParams(dimension_semantics=("arbitrary",) * n_axes, vmem_limit_bytes=vmem)


def _pick(n, pref):
    t = min(n, pref)
    while n % t:
        t //= 2
    return t


def _ln_rows(y, g, b):
    mu = jnp.mean(y, axis=-1, keepdims=True)
    yc = y - mu
    var = jnp.mean(yc * yc, axis=-1, keepdims=True)
    return yc * lax.rsqrt(var + LN_EPS) * g + b


def _ln_in_kernel(x_ref, g_ref, b_ref, xo_ref):
    xo_ref[...] = _ln_rows(x_ref[...], g_ref[...], b_ref[...])


def _ln_in(x, g, b):
    M, D = x.shape
    tm = _pick(M, 256)
    row = pl.BlockSpec((tm, D), lambda i: (i, 0))
    vec = pl.BlockSpec((1, D), lambda i: (0, 0))
    return pl.pallas_call(
        _ln_in_kernel, name="ln_in",
        grid=(M // tm,),
        in_specs=[row, vec, vec],
        out_specs=row,
        out_shape=jax.ShapeDtypeStruct((M, D), F32),
        compiler_params=_params(1),
    )(x, g.reshape(1, D), b.reshape(1, D))


def _cast_rows(x_ref, xb_ref):
    @pl.when(pl.program_id(1) == 0)
    def _():
        xb_ref[...] = x_ref[...].astype(BF16)


def _mm_kernel(x_ref, w_ref, o_ref, xb_ref, *maybe_r_ref, dil):
    _cast_rows(x_ref, xb_ref)
    r = jnp.dot(xb_ref[...], w_ref[...], preferred_element_type=F32)
    if dil == 1:
        o_ref[0] = r.astype(o_ref.dtype)
    else:
        r_ref, = maybe_r_ref
        rows = r_ref.shape[1] // dil
        for c in range(r_ref.shape[0]):
            cols = slice(c * LANES, (c + 1) * LANES)
            r_ref[c] = r[:, cols]
            for rho in range(dil):
                o_ref[rho, :, cols] = r_ref[c, pl.ds(rho, rows, stride=dil), :].astype(o_ref.dtype)


def _matmul(x, w, dil, tm, tn, name):
    M, K = x.shape
    N = w.shape[1]
    tm, tn = _pick(M, tm), _pick(N, tn)
    return pl.pallas_call(
        functools.partial(_mm_kernel, dil=dil), name=name,
        grid=(M // tm, N // tn),
        in_specs=[pl.BlockSpec((tm, K), lambda i, j: (i, 0)), pl.BlockSpec((K, tn), lambda i, j: (0, j))],
        out_specs=pl.BlockSpec((dil, tm // dil, tn), lambda i, j: (0, i, j)),
        out_shape=jax.ShapeDtypeStruct((dil, M // dil, N), BF16),
        scratch_shapes=[pltpu.VMEM((tm, K), BF16)] + ([pltpu.VMEM((tn // LANES, tm, LANES), F32)] if dil > 1 else []),
        compiler_params=_params(2),
    )(x, w)


def _swiglu_kernel(x_ref, wa_ref, wb_ref, o_ref, xb_ref):
    _cast_rows(x_ref, xb_ref)
    x = xb_ref[...]
    a = jnp.dot(x, wa_ref[...], preferred_element_type=F32)
    b = jnp.dot(x, wb_ref[...], preferred_element_type=F32)
    o_ref[...] = (a * (1.0 / (1.0 + jnp.exp(-a))) * b).astype(o_ref.dtype)


def _swiglu(x, w, tm, tn):
    M, K = x.shape
    F = w.shape[1] // 2
    tm, tn = _pick(M, tm), _pick(F, tn)
    nf = F // tn
    return pl.pallas_call(
        _swiglu_kernel, name="swiglu",
        grid=(M // tm, nf),
        in_specs=[pl.BlockSpec((tm, K), lambda i, j: (i, 0)),
                  pl.BlockSpec((K, tn), lambda i, j: (0, j)),
                  pl.BlockSpec((K, tn), lambda i, j: (0, j + nf))],
        out_specs=pl.BlockSpec((tm, tn), lambda i, j: (i, j)),
        out_shape=jax.ShapeDtypeStruct((M, F), BF16),
        scratch_shapes=[pltpu.VMEM((tm, K), BF16)],
        compiler_params=_params(2),
    )(x, w, w)


MM_LN_COL_CHUNK = 1024
MM_LN_ROW_CHUNK = 128


def _mm_res_ln_kernel(a_ref, w_ref, xres_ref, g_ref, b_ref, xo_ref, *, nk, alpha):
    k = pl.program_id(1)
    tm, D = xo_ref.shape

    @pl.when(k == 0)
    def _():
        xo_ref[...] = jnp.zeros_like(xo_ref)

    a = a_ref[...]
    cw = _pick(D, MM_LN_COL_CHUNK)
    for c in range(D // cw):
        cols = slice(c * cw, (c + 1) * cw)
        xo_ref[:, cols] += jnp.dot(a, w_ref[:, cols], preferred_element_type=F32)

    @pl.when(k == nk - 1)
    def _():
        rw = _pick(tm, MM_LN_ROW_CHUNK)
        for r in range(tm // rw):
            rows = slice(r * rw, (r + 1) * rw)
            xo_ref[rows, :] = _ln_rows(alpha * xres_ref[rows, :] + xo_ref[rows, :], g_ref[...], b_ref[...])


def _mm_res_ln(a, w, xres, g, b, alpha, tm, tk, name):
    M, K = a.shape
    D = w.shape[1]
    tm, tk = _pick(M, tm), _pick(K, tk)
    nk = K // tk
    row = pl.BlockSpec((tm, D), lambda i, k: (i, 0))
    vec = pl.BlockSpec((1, D), lambda i, k: (0, 0))
    return pl.pallas_call(
        functools.partial(_mm_res_ln_kernel, nk=nk, alpha=alpha), name=name,
        grid=(M // tm, nk),
        in_specs=[pl.BlockSpec((tm, tk), lambda i, k: (i, k)),
                  pl.BlockSpec((tk, D), lambda i, k: (k, 0)),
                  row, vec, vec],
        out_specs=row,
        out_shape=jax.ShapeDtypeStruct((M, D), F32),
        compiler_params=_params(2),
    )(a, w, xres, g.reshape(1, D), b.reshape(1, D))


def _gate_merge_kernel(gz_ref, oa_ref, ob_ref, oc_ref, od_ref, wup_ref, bg_ref, wbr_ref, o_ref):
    gz = gz_ref[...]
    acc = None
    for n, o_n in enumerate((oa_ref, ob_ref, oc_ref, od_ref)):
        z = jnp.dot(gz, wup_ref[n], preferred_element_type=F32) + bg_ref[n:n + 1, :]
        t = jnp.dot(o_n[...], wbr_ref[n], preferred_element_type=F32)
        term = t * (1.0 / (1.0 + jnp.exp(-z)))
        acc = term if acc is None else acc + term
    o_ref[...] = acc.astype(o_ref.dtype)


def _gate_merge(h, o_a, o_b, o_c, o_d, wup, bg, wbr, tm, tn):
    M = h.shape[0]
    D = wup.shape[2]
    tm, tn = _pick(M, tm), _pick(D, tn)
    o_spec = pl.BlockSpec((tm, BRANCH_W), lambda i, j: (i, 0))
    return pl.pallas_call(
        _gate_merge_kernel, name="gate_merge",
        grid=(M // tm, D // tn),
        in_specs=[pl.BlockSpec((tm, GATE_RANK), lambda i, j: (i, COL_GZ // GATE_RANK)),
                  o_spec, o_spec, o_spec, o_spec,
                  pl.BlockSpec((N_BRANCH, GATE_RANK, tn), lambda i, j: (0, 0, j)),
                  pl.BlockSpec((N_BRANCH, tn), lambda i, j: (0, j)),
                  pl.BlockSpec((N_BRANCH, BRANCH_W, tn), lambda i, j: (0, 0, j))],
        out_specs=pl.BlockSpec((tm, tn), lambda i, j: (i, j)),
        out_shape=jax.ShapeDtypeStruct((M, D), BF16),
        compiler_params=_params(2),
    )(h, o_a, o_b, o_c, o_d, wup, bg, wbr)


def _lane():
    return lax.broadcasted_iota(jnp.int32, (1, LANES), 1)


def _dot_nt(a, b):
    return lax.dot_general(a, b, (((1,), (1,)), ((), ())), preferred_element_type=F32)


def _split_heads(x):
    lo = (_lane() < HEAD_DIM).astype(F32)
    return jnp.concatenate([(x * lo).astype(BF16), (x * (1.0 - lo)).astype(BF16)], axis=0)


def _merge_heads(o, rows):
    return jnp.where(_lane() < HEAD_DIM, o[:rows], o[rows:])


def _group_mean(x):
    r = lax.broadcasted_iota(jnp.int32, (LANES, LANES), 0) >> 6
    c = lax.broadcasted_iota(jnp.int32, (LANES, LANES), 1) >> 6
    p = jnp.where(r == c, 1.0 / HEAD_DIM, 0.0).astype(BF16)
    hi = x.astype(BF16)
    lo = (x - hi.astype(F32)).astype(BF16)
    return jnp.dot(hi, p, preferred_element_type=F32) + jnp.dot(lo, p, preferred_element_type=F32)


def _rms_heads(x, g):
    return x * lax.rsqrt(_group_mean(x * x) + RMS_EPS) * g


FLASH_ROWS = 32
LOG2E = math.log2(math.e)


def _flash_scratch(rows, tk):
    wide = lambda dt: pltpu.VMEM((rows, tk), dt)
    narrow = pltpu.VMEM((rows, LANES), F32)
    return [wide(F32), wide(F32), wide(BF16), wide(BF16)] + [narrow] * 5


def _flash(qs, k_ref, v_ref, scratch, n_chunks, tk, bias_fn=None):
    s_refs, p_refs, a_refs = scratch[0:2], scratch[2:4], scratch[4:6]
    m_ref, l_ref, acc_ref = scratch[6:9]
    rows = qs.shape[0]
    rb = FLASH_ROWS
    assert n_chunks % 2 == 0 and rows % rb == 0
    m_ref[...] = jnp.full(m_ref.shape, NEG_INF, F32)
    l_ref[...] = jnp.zeros(l_ref.shape, F32)
    acc_ref[...] = jnp.zeros(acc_ref.shape, F32)
    p_refs[1][...] = jnp.zeros(p_refs[1].shape, BF16)
    a_refs[1][...] = jnp.ones(a_refs[1].shape, F32)

    def qk(t, dst):
        dst[...] = _dot_nt(qs, k_ref[pl.ds(pl.multiple_of(t * tk, tk), tk), :])

    def pv(t, par):
        vb = v_ref[pl.ds(pl.multiple_of(t * tk, tk), tk), :]
        acc_ref[...] = a_refs[par][...] * acc_ref[...] + jnp.dot(p_refs[par][...], vb, preferred_element_type=F32)

    def softmax(t, par):
        s_ref, p_ref, a_ref = s_refs[par], p_refs[par], a_refs[par]
        ncol = tk // LANES
        for i in range(rows // rb):
            rs = slice(i * rb, (i + 1) * rb)
            s = [s_ref[rs, c * LANES:(c + 1) * LANES] for c in range(ncol)]
            if bias_fn is not None:
                s = [sc + bias_fn(t, i * rb, c) for c, sc in enumerate(s)]
                for c in range(ncol):
                    s_ref[rs, c * LANES:(c + 1) * LANES] = s[c]
            m_old = m_ref[rs, :]
            mx = functools.reduce(jnp.maximum, s)
            m_new = jnp.maximum(m_old, jnp.broadcast_to(jnp.max(mx, axis=-1, keepdims=True), (rb, LANES)))
            m_ref[rs, :] = m_new
            a_ref[rs, :] = jnp.exp2(m_old - m_new)
        for i in range(rows // rb):
            rs = slice(i * rb, (i + 1) * rb)
            m_new = m_ref[rs, :]
            p = [jnp.exp2(s_ref[rs, c * LANES:(c + 1) * LANES] - m_new) for c in range(ncol)]
            l_ref[rs, :] = a_ref[rs, :] * l_ref[rs, :] + functools.reduce(jnp.add, p)
            for c in range(ncol):
                p_ref[rs, c * LANES:(c + 1) * LANES] = p[c].astype(BF16)

    def step(t, par):
        qk(jnp.minimum(t + 1, n_chunks - 1), s_refs[1 - par])
        pv(jnp.maximum(t - 1, 0), 1 - par)
        softmax(t, par)

    def pair(j, carry):
        step(2 * j, 0)
        step(2 * j + 1, 1)
        return carry

    qk(0, s_refs[0])
    lax.fori_loop(0, n_chunks // 2, pair, 0)
    pv(n_chunks - 1, 1)
    return acc_ref[...] / jnp.sum(l_ref[...], axis=-1, keepdims=True)


def _t5_bucket(rel):
    half = T5_BUCKETS // 2
    max_exact = half // 2
    n = jnp.abs(rel)
    nf = jnp.maximum(n, 1).astype(jnp.float32)
    large = max_exact + (jnp.log(nf / max_exact) / math.log(T5_MAX_DIST / max_exact)
                         * (half - max_exact)).astype(jnp.int32)
    large = jnp.minimum(large, half - 1)
    return jnp.where(rel > 0, half, 0) + jnp.where(n < max_exact, n, large)


def _toeplitz(x, rows, cols):
    n = rows + cols
    pad = n - x.shape[-1]
    x = x[..., :n] if pad <= 0 else jnp.pad(x, [(0, 0)] * (x.ndim - 1) + [(0, pad)])
    t = jnp.tile(x, (1,) * (x.ndim - 1) + (rows,))[..., :rows * (n - 1)]
    return t.reshape(x.shape[:-1] + (rows, n - 1))[..., rows - 1:rows - 1 + cols]


def _na_bias_table(rpb):
    H = rpb.shape[0]
    qcol = jnp.arange(GRID_W, dtype=jnp.int32)
    kcol = jnp.arange(GRID_W, dtype=jnp.int32)
    c0 = jnp.clip(qcol - NA_COLS // 2, 0, GRID_W - NA_COLS)
    col_ok = (kcol[None, :] >= c0[:, None]) & (kcol[None, :] < c0[:, None] + NA_COLS)
    rel = jnp.arange(-(GRID_W - 1), GRID_W, dtype=jnp.int32)
    by_rel = rpb[:, :, jnp.clip(rel + NA_COLS - 1, 0, 2 * NA_COLS - 2)]
    rpb_col = jnp.where(col_ok, _toeplitz(by_rel, GRID_W, GRID_W), NEG_INF)
    t = jnp.stack([rpb_col[:, NA_ROWS - 1 - d:2 * NA_ROWS - 1 - d] for d in range(NA_ROWS)])
    return t.transpose(0, 1, 3, 2, 4).reshape(NA_ROWS, H, GRID_W, NA_ROWS * GRID_W).astype(F32)


def _na_kernel(q_ref, k_ref, v_ref, bias_ref, o_ref, *, R):
    r = pl.program_id(1)
    r0 = jnp.clip(r - NA_ROWS // 2, 0, R - NA_ROWS)
    delta = r - r0
    ks = pl.multiple_of(r0 * GRID_W, GRID_W)
    nk = NA_ROWS * GRID_W
    for c in range(BRANCH_W // LANES):
        cols = slice(c * LANES, (c + 1) * LANES)
        qs = _split_heads(q_ref[:, cols].astype(F32) * HEAD_DIM ** -0.5)
        s = _dot_nt(qs, k_ref[pl.ds(ks, nk), cols])
        s = s + jnp.concatenate([bias_ref[delta, 2 * c], bias_ref[delta, 2 * c + 1]], axis=0)
        m = jnp.max(s, axis=-1, keepdims=True)
        p = jnp.exp(s - m)
        l = jnp.sum(p, axis=-1, keepdims=True)
        o = jnp.dot(p.astype(BF16), v_ref[pl.ds(ks, nk), cols], preferred_element_type=F32) / l
        o_ref[:, cols] = _merge_heads(o, GRID_W).astype(o_ref.dtype)


def _na_attention(h, bias, B, T):
    R = T // GRID_W
    assert R >= NA_ROWS
    M = h.shape[0]
    return pl.pallas_call(
        functools.partial(_na_kernel, R=R), name="na_attn",
        grid=(B, R),
        in_specs=[pl.BlockSpec((GRID_W, BRANCH_W), lambda b, r: (b * R + r, COL_QA // BRANCH_W)),
                  pl.BlockSpec((T, BRANCH_W), lambda b, r: (b, COL_KA // BRANCH_W)),
                  pl.BlockSpec((T, BRANCH_W), lambda b, r: (b, COL_VA // BRANCH_W)),
                  pl.BlockSpec(bias.shape, lambda b, r: (0, 0, 0, 0))],
        out_specs=pl.BlockSpec((GRID_W, BRANCH_W), lambda b, r: (b * R + r, 0)),
        out_shape=jax.ShapeDtypeStruct((M, BRANCH_W), BF16),
        compiler_params=_params(2),
    )(h, h, h, bias)


DIFF_TQ, DIFF_TK = 256, 512
DIFF_D_LO = -(DIFF_TK + DIFF_TQ)
DIFF_D_HI = 2 * DIFF_TQ
assert DIFF_TQ >= T5_MAX_DIST and DIFF_TK % DIFF_TQ == 0


def _diff_bias_tiles(t5_tab):
    tq, tk = DIFF_TQ, DIFF_TK
    lo = DIFF_D_LO - (tq - 1)
    rel = jnp.arange(lo, DIFF_D_HI + tk, dtype=jnp.int32)
    by_rel = t5_tab[_t5_bucket(rel)].T
    offs = range(DIFF_D_LO, DIFF_D_HI + 1, tq)
    x = jnp.stack([by_rel[:, d - (tq - 1) - lo:d + tk - lo] for d in offs], axis=1)
    return (_toeplitz(x, tq, tk) * LOG2E).astype(F32)


def _diff_kernel(lam_ref, q_ref, k_ref, v_ref, tiles_ref, g_ref, o_ref, *scratch, n_chunks, out_scale):
    qi = pl.program_id(2)
    tq, tk = DIFF_TQ, DIFF_TK
    lane = _lane()
    qf = q_ref[...].astype(F32) * (DIFF_D ** -0.5 * LOG2E)
    qs = jnp.concatenate([(qf * ((lane >> 5) == v).astype(F32)).astype(BF16) for v in range(4)], axis=0)

    def bias_fn(kc, r0, c):
        d = jnp.clip(kc * tk - qi * tq, DIFF_D_LO, DIFF_D_HI)
        j = (d - DIFF_D_LO) // tq
        return tiles_ref[r0 // (2 * tq), j, r0 % tq:r0 % tq + FLASH_ROWS, c * LANES:(c + 1) * LANES]

    o = _flash(qs, k_ref, v_ref, scratch, n_chunks, tk, bias_fn)
    lam = lam_ref[0]
    x = jnp.where(lane < HEAD_DIM, o[:tq] - lam * o[tq:2 * tq], o[2 * tq:3 * tq] - lam * o[3 * tq:])
    o_ref[...] = (_rms_heads(x, g_ref[...]) * out_scale).astype(o_ref.dtype)


def _diff_attention(h, tiles, lam, g, lam_init, B, T):
    M = h.shape[0]
    tq, tk = DIFF_TQ, DIFF_TK
    assert T % tk == 0
    nq = T // tq
    qb, kb, vb = COL_QB // LANES, COL_KB // LANES, COL_VB // LANES
    g2 = jnp.tile(g.astype(F32), 2).reshape(1, LANES)
    return pl.pallas_call(
        functools.partial(_diff_kernel, n_chunks=T // tk, out_scale=1.0 - lam_init), name="diff_attn",
        grid=(B, DIFF_HEADS // 2, nq),
        in_specs=[pl.BlockSpec(memory_space=pltpu.SMEM),
                  pl.BlockSpec((tq, LANES), lambda b, hp, qi: (b * nq + qi, qb + hp)),
                  pl.BlockSpec((T, LANES), lambda b, hp, qi: (b, kb + hp)),
                  pl.BlockSpec((T, LANES), lambda b, hp, qi: (b, vb + hp)),
                  pl.BlockSpec((2,) + tiles.shape[1:], lambda b, hp, qi: (hp, 0, 0, 0)),
                  pl.BlockSpec((1, LANES), lambda b, hp, qi: (0, 0))],
        out_specs=pl.BlockSpec((tq, LANES), lambda b, hp, qi: (b * nq + qi, hp)),
        out_shape=jax.ShapeDtypeStruct((M, BRANCH_W), BF16),
        scratch_shapes=_flash_scratch(4 * tq, tk),
        compiler_params=_params(3),
    )(lam.reshape(1).astype(F32), h, h, h, tiles, g2)


GQA_TQ, GQA_TK = 128, 512


def _rope_tables(T):
    t = jnp.arange(T, dtype=jnp.int32)
    row = (t // GRID_W).astype(jnp.float32)
    col = (t % GRID_W).astype(jnp.float32)
    freqs = ROPE_THETA ** (-jnp.arange(0, ROPE_AXIS_DIM, 2, dtype=jnp.float32) / ROPE_AXIS_DIM)
    ang = jnp.concatenate([row[:, None] * freqs, col[:, None] * freqs], axis=-1)
    cos, sin = jnp.cos(ang), jnp.sin(ang)
    cos_t = jnp.tile(cos, (1, 4))
    sin_t = jnp.tile(jnp.concatenate([-sin, sin], axis=-1), (1, 2))
    return cos_t.astype(F32), sin_t.astype(F32)


def _rope(x, cos, sin):
    first = (_lane() & (HEAD_DIM - 1)) < HEAD_DIM // 2
    partner = jnp.where(first, pltpu.roll(x, LANES - HEAD_DIM // 2, axis=1), pltpu.roll(x, HEAD_DIM // 2, axis=1))
    return x * cos + partner * sin


def _gqa_kernel(q_ref, k_ref, v_ref, cq_ref, sq_ref, ck_ref, sk_ref, gq_ref, gk_ref, o_ref, kproc_ref, *scratch,
                n_chunks):
    qi = pl.program_id(1)
    tq = GQA_TQ

    @pl.when(qi == 0)
    def _():
        kf = _rms_heads(k_ref[...].astype(F32), gk_ref[...])
        kproc_ref[...] = _rope(kf, ck_ref[...], sk_ref[...]).astype(BF16)

    cq, sq, gq = cq_ref[...], sq_ref[...], gq_ref[...]
    parts = []
    for c in range(BRANCH_W // LANES):
        qf = _rms_heads(q_ref[:, c * LANES:(c + 1) * LANES].astype(F32), gq)
        parts.append(_split_heads(_rope(qf, cq, sq) * (HEAD_DIM ** -0.5 * LOG2E)))
    o = _flash(jnp.concatenate(parts, axis=0), kproc_ref, v_ref, scratch, n_chunks, GQA_TK)
    for c in range(BRANCH_W // LANES):
        o_ref[:, c * LANES:(c + 1) * LANES] = _merge_heads(o[2 * c * tq:(2 * c + 2) * tq], tq).astype(o_ref.dtype)


def _gqa_attention(h, cos_t, sin_t, gq, gk, B, T):
    M = h.shape[0]
    tq, tk = GQA_TQ, GQA_TK
    assert T % tk == 0
    nq = T // tq
    vec = pl.BlockSpec((1, LANES), lambda b, qi: (0, 0))
    qtab = pl.BlockSpec((tq, LANES), lambda b, qi: (qi, 0))
    ktab = pl.BlockSpec((T, LANES), lambda b, qi: (0, 0))
    return pl.pallas_call(
        functools.partial(_gqa_kernel, n_chunks=T // tk), name="gqa_attn",
        grid=(B, nq),
        in_specs=[pl.BlockSpec((tq, BRANCH_W), lambda b, qi: (b * nq + qi, COL_QC // BRANCH_W)),
                  pl.BlockSpec((T, LANES), lambda b, qi: (b, COL_KC // LANES)),
                  pl.BlockSpec((T, LANES), lambda b, qi: (b, COL_VC // LANES)),
                  qtab, qtab, ktab, ktab, vec, vec],
        out_specs=pl.BlockSpec((tq, BRANCH_W), lambda b, qi: (b * nq + qi, 0)),
        out_shape=jax.ShapeDtypeStruct((M, BRANCH_W), BF16),
        scratch_shapes=[pltpu.VMEM((T, LANES), BF16)] + _flash_scratch(GQA_HEADS * tq, tk),
        compiler_params=_params(2),
    )(h, h, h, cos_t, sin_t, cos_t, sin_t, gq, gk)


DIL_TQ = 128
DIL_HALF = 64
assert all(w // (2 * d) == DIL_HALF for w, d in DIL_GROUPS)


def _dil_bias_tiles(t5_tab, g, L):
    dil = DIL_GROUPS[g][1]
    tq = DIL_TQ
    W = min(2 * tq, L)
    offs = range(0, -(W - tq) - 1, -DIL_HALF)
    lo = offs[-1] - (tq - 1)
    rel = jnp.arange(lo, W, dtype=jnp.int32)
    s0 = DIFF_HEADS + g * DIL_SLOTS
    by_rel = t5_tab[:, s0:s0 + DIL_SLOTS][_t5_bucket(rel * dil)]
    by_rel = jnp.where((jnp.abs(rel) <= DIL_HALF)[:, None], by_rel, NEG_INF).T
    x = jnp.stack([by_rel[:, d - (tq - 1) - lo:d + W - lo] for d in offs])
    return _toeplitz(x, tq, W).astype(F32)


def _dil_kernel(q_ref, k_ref, v_ref, tiles_ref, o_ref, lse_ref, *, L, W):
    n = pl.program_id(2)
    tq = DIL_TQ
    ws = jnp.clip(n * tq - DIL_HALF, 0, L - W)
    var = (n * tq - ws) // DIL_HALF
    ws = pl.multiple_of(ws, DIL_HALF)
    for c in range(BRANCH_W // LANES):
        cols = slice(c * LANES, (c + 1) * LANES)
        qs = _split_heads(q_ref[:, cols].astype(F32) * HEAD_DIM ** -0.5)
        s = _dot_nt(qs, k_ref[pl.ds(ws, W), cols])
        s = s + jnp.concatenate([tiles_ref[var, 2 * c], tiles_ref[var, 2 * c + 1]], axis=0)
        m = jnp.max(s, axis=-1, keepdims=True)
        p = jnp.exp(s - m)
        l = jnp.sum(p, axis=-1, keepdims=True)
        o = jnp.dot(p.astype(BF16), v_ref[pl.ds(ws, W), cols], preferred_element_type=F32) / l
        lse = jnp.broadcast_to(m + jnp.log(l), (2 * tq, LANES))
        o_ref[:, cols] = _merge_heads(o, tq).astype(o_ref.dtype)
        lse_ref[:, cols] = _merge_heads(lse, tq)


def _dil_attention(hd, col0, tiles, B, T):
    dil, Md, _ = hd.shape
    L = T // dil
    tq = DIL_TQ
    W = min(2 * tq, L)
    assert L % tq == 0
    nq = L // tq
    out_spec = pl.BlockSpec((None, tq, BRANCH_W), lambda b, rho, n: (rho, b * nq + n, 0))
    return pl.pallas_call(
        functools.partial(_dil_kernel, L=L, W=W), name=f"dil_attn_{dil}",
        grid=(B, dil, nq),
        in_specs=[pl.BlockSpec((None, tq, BRANCH_W), lambda b, rho, n: (rho, b * nq + n, col0)),
                  pl.BlockSpec((None, L, BRANCH_W), lambda b, rho, n: (rho, b, col0 + 1)),
                  pl.BlockSpec((None, L, BRANCH_W), lambda b, rho, n: (rho, b, col0 + 2)),
                  pl.BlockSpec(tiles.shape, lambda b, rho, n: (0, 0, 0, 0))],
        out_specs=[out_spec, out_spec],
        out_shape=[jax.ShapeDtypeStruct((dil, Md, BRANCH_W), BF16), jax.ShapeDtypeStruct((dil, Md, BRANCH_W), F32)],
        compiler_params=_params(3),
    )(hd, hd, hd, tiles)


def _dil_merge_kernel(o0_ref, o1_ref, o2_ref, l0_ref, l1_ref, l2_ref, o_ref, *scratch):
    def natural(ref, scr):
        dil, rows, _ = ref.shape
        for c in range(scr.shape[0]):
            for rho in range(dil):
                scr[c, pl.ds(rho, rows, stride=dil), :] = ref[rho, :, c * LANES:(c + 1) * LANES].astype(F32)
        return jnp.concatenate([scr[c] for c in range(scr.shape[0])], axis=1)

    os_ = [o0_ref[0].astype(F32), natural(o1_ref, scratch[0]), natural(o2_ref, scratch[1])]
    ls = [l0_ref[0], natural(l1_ref, scratch[2]), natural(l2_ref, scratch[3])]
    mx = jnp.maximum(jnp.maximum(ls[0], ls[1]), ls[2])
    es = [jnp.exp(l - mx) for l in ls]
    den = es[0] + es[1] + es[2]
    num = es[0] * os_[0] + es[1] * os_[1] + es[2] * os_[2]
    o_ref[...] = (num / den).astype(o_ref.dtype)


def _dil_merge(outs, lses):
    M = outs[0].shape[1]
    tm = _pick(M, 512)

    def spec(a):
        dil = a.shape[0]
        return pl.BlockSpec((dil, tm // dil, BRANCH_W), lambda i: (0, i, 0))

    return pl.pallas_call(
        _dil_merge_kernel, name="dil_merge",
        grid=(M // tm,),
        in_specs=[spec(a) for a in (*outs, *lses)],
        out_specs=pl.BlockSpec((tm, BRANCH_W), lambda i: (i, 0)),
        out_shape=jax.ShapeDtypeStruct((M, BRANCH_W), BF16),
        scratch_shapes=[pltpu.VMEM((BRANCH_W // LANES, tm, LANES), F32)] * 4,
        compiler_params=_params(1),
    )(*outs, *lses)


def _deinterleave(w, n_heads):
    d = w.shape[0]
    return w.reshape(d, n_heads, HEAD_DIM // 2, 2).transpose(0, 1, 3, 2).reshape(d, n_heads * HEAD_DIM)


def _prep_layer(l, w_in, w_gate_down, w_gate_up, w_branch, w_out, w_ffn_in, w_ffn_out, ff_pad):
    d = w_in.shape[1]
    wi = w_in[l]
    qc = _deinterleave(wi[:, 3072:3584], GQA_HEADS).reshape(d, GQA_HEADS, HEAD_DIM)
    qc = qc[:, jnp.array(GQA_HEAD_ORDER)].reshape(d, GQA_HEADS * HEAD_DIM)
    kc = _deinterleave(wi[:, 3584:3712], GQA_KV_HEADS)
    qd, kd, vd = wi[:, 3840:5376], wi[:, 5376:6912], wi[:, 6912:8448]

    def group(g):
        s = slice(g * BRANCH_W, (g + 1) * BRANCH_W)
        return [qd[:, s], kd[:, s], vd[:, s]]

    w_main = jnp.concatenate([wi[:, :3072], qc, kc, wi[:, 3712:3840], w_gate_down[l]] + group(0), axis=1)
    assert w_main.shape[1] == H_MAIN_COLS
    wup = w_gate_up[l].reshape(GATE_RANK, N_BRANCH, d).transpose(1, 0, 2)
    wbr = w_branch[l]
    wbr_c = wbr[2].reshape(GQA_HEADS, HEAD_DIM, d)[jnp.array(GQA_HEAD_ORDER)].reshape(BRANCH_W, d)
    wbr = jnp.stack([wbr[0], wbr[1], wbr_c, wbr[3]])
    ff = w_ffn_out.shape[1]
    wfi = w_ffn_in[l]
    pad = ff_pad - ff
    wfi = jnp.concatenate([jnp.pad(wfi[:, :ff], ((0, 0), (0, pad))), jnp.pad(wfi[:, ff:], ((0, 0), (0, pad)))], axis=1)
    wfo = jnp.pad(w_ffn_out[l], ((0, pad), (0, 0)))
    return dict(w_main=w_main.astype(BF16), w_d1=jnp.concatenate(group(1), axis=1).astype(BF16),
                w_d2=jnp.concatenate(group(2), axis=1).astype(BF16), wup=wup.astype(BF16), wbr=wbr.astype(BF16),
                w_out=w_out[l].astype(BF16), wfi=wfi.astype(BF16), wfo=wfo.astype(BF16))


def _trunk(x, consts, layers, p):
    B, T, D = x.shape
    M = B * T
    depth = len(layers)
    alpha = (2 * depth) ** 0.25
    cos_t, sin_t = _rope_tables(T)
    dil_tiles = [_dil_bias_tiles(p['t5_table'], g, T // dil) for g, (_, dil) in enumerate(DIL_GROUPS)]
    xf = _ln_in(x.reshape(M, D), p['ln_in_g'], p['ln_in_b'])
    for l, lw in enumerate(layers):
        lam_init = 0.8 - 0.6 * math.exp(-0.3 * l)
        h3 = _matmul(xf, lw['w_main'], 1, 512, 512, "proj_main")
        h = h3[0]
        hd1 = _matmul(xf, lw['w_d1'], DIL_GROUPS[1][1], 512, 512, "proj_dil4")
        hd2 = _matmul(xf, lw['w_d2'], DIL_GROUPS[2][1], 512, 512, "proj_dil16")
        o_a = _na_attention(h, consts['na_bias'][l], B, T)
        lq = p['diff_lambda'][l].astype(F32)
        lam = jnp.exp(jnp.sum(lq[0] * lq[1])) - jnp.exp(jnp.sum(lq[2] * lq[3])) + lam_init
        o_b = _diff_attention(h, consts['diff_tiles'], lam, p['diff_subln_g'][l], lam_init, B, T)
        gq = jnp.tile(jnp.concatenate([p['qk_norm_g'][l, 0, 0::2], p['qk_norm_g'][l, 0, 1::2]]), 2).reshape(1, LANES)
        gk = jnp.tile(jnp.concatenate([p['qk_norm_g'][l, 1, 0::2], p['qk_norm_g'][l, 1, 1::2]]), 2).reshape(1, LANES)
        o_c = _gqa_attention(h, cos_t, sin_t, gq.astype(F32), gk.astype(F32), B, T)
        dil_out = [_dil_attention(h3, COL_D0 // BRANCH_W, dil_tiles[0], B, T),
                   _dil_attention(hd1, 0, dil_tiles[1], B, T),
                   _dil_attention(hd2, 0, dil_tiles[2], B, T)]
        o_d = _dil_merge([o for o, _ in dil_out], [s for _, s in dil_out])
        merged = _gate_merge(h, o_a, o_b, o_c, o_d, lw['wup'], p['b_gate'][l].astype(F32), lw['wbr'], 512, 1024)
        xf = _mm_res_ln(merged, lw['w_out'], xf, p['ln1_g'][l], p['ln1_b'][l], alpha, 512, 512, "out_proj_ln")
        ff = _swiglu(xf, lw['wfi'], 512, 512)
        xf = _mm_res_ln(ff, lw['wfo'], xf, p['ln2_g'][l], p['ln2_b'][l], alpha, 512, 512, "ffn_out_ln")
    return xf.reshape(B, T, D)


def kernel(x_prompt, x_sample, ln_in_g, ln_in_b, w_in, na_rpb, qk_norm_g, diff_lambda, diff_subln_g, t5_table,
           w_gate_down, w_gate_up, b_gate, w_branch, w_out, ln1_g, ln1_b, w_ffn_in, w_ffn_out, ln2_g, ln2_b):
    depth = w_in.shape[0]
    ff = w_ffn_out.shape[1]
    ff_pad = -(-ff // 1024) * 1024
    layers = [_prep_layer(l, w_in, w_gate_down, w_gate_up, w_branch, w_out, w_ffn_in, w_ffn_out, ff_pad)
              for l in range(depth)]
    consts = dict(na_bias=[_na_bias_table(na_rpb[l]) for l in range(depth)],
                  diff_tiles=_diff_bias_tiles(t5_table[:, :DIFF_HEADS]))
    p = dict(ln_in_g=ln_in_g, ln_in_b=ln_in_b, qk_norm_g=qk_norm_g, diff_lambda=diff_lambda,
             diff_subln_g=diff_subln_g, t5_table=t5_table, b_gate=b_gate, ln1_g=ln1_g, ln1_b=ln1_b,
             ln2_g=ln2_g, ln2_b=ln2_b)
    return (_trunk(x_prompt, consts, layers, p), _trunk(x_sample, consts, layers, p))
```

```python
import functools
import math

import jax
import jax.numpy as jnp
from jax import lax
from jax.experimental import pallas as pl
from jax.experimental.pallas import tpu as pltpu

F32 = jnp.float32
BF16 = jnp.bfloat16

HEAD_DIM = 64
LANES = 128
GRID_W = 64
NA_ROWS = 8
NA_COLS = 16
DIFF_HEADS = 8
DIFF_D = HEAD_DIM // 2
GQA_HEADS = 8
GQA_KV_HEADS = 2
ROPE_AXIS_DIM = HEAD_DIM // 2
ROPE_THETA = 10000.0
DIL_SLOTS = 8
DIL_GROUPS = ((128, 1), (512, 4), (2048, 16))
T5_BUCKETS = 32
T5_MAX_DIST = 128
N_BRANCH = 4
BRANCH_W = 8 * HEAD_DIM
GATE_RANK = 256
LN_EPS = 1e-5
RMS_EPS = 1e-6
NEG_INF = -1e30

COL_QA, COL_KA, COL_VA = 0, 512, 1024
COL_QB, COL_KB, COL_VB = 1536, 2048, 2560
COL_QC = 3072
COL_KC, COL_VC = 3584, 3712
COL_GZ = 3840
COL_D0 = 4096
H_MAIN_COLS = COL_D0 + 3 * BRANCH_W
GQA_HEAD_ORDER = (0, 4, 1, 5, 2, 6, 3, 7)

VMEM_LIMIT = 56 * 1024 * 1024


def _params(n_axes, vmem=VMEM_LIMIT):
    return pltpu.CompilerParams(dimension_semantics=("arbitrary",) * n_axes, vmem_limit_bytes=vmem)


def _pick(n, pref):
    t = min(n, pref)
    while n % t:
        t //= 2
    return t


def _ln_rows(y, g, b):
    mu = jnp.mean(y, axis=-1, keepdims=True)
    yc = y - mu
    var = jnp.mean(yc * yc, axis=-1, keepdims=True)
    return yc * lax.rsqrt(var + LN_EPS) * g + b


def _ln_in_kernel(x_ref, g_ref, b_ref, xo_ref):
    xo_ref[...] = _ln_rows(x_ref[...], g_ref[...], b_ref[...])


def _ln_in(x, g, b):
    M, D = x.shape
    tm = _pick(M, 256)
    row = pl.BlockSpec((tm, D), lambda i: (i, 0))
    vec = pl.BlockSpec((1, D), lambda i: (0, 0))
    return pl.pallas_call(
        _ln_in_kernel, name="ln_in",
        grid=(M // tm,),
        in_specs=[row, vec, vec],
        out_specs=row,
        out_shape=jax.ShapeDtypeStruct((M, D), F32),
        compiler_params=_params(1),
    )(x, g.reshape(1, D), b.reshape(1, D))


def _cast_rows(x_ref, xb_ref):
    @pl.when(pl.program_id(1) == 0)
    def _():
        xb_ref[...] = x_ref[...].astype(BF16)


def _proj_kernel(x_ref, w_ref, om_ref, o1_ref, o2_ref, xb_ref, r_ref, *, n_main, n_dil):
    j = pl.program_id(1)
    _cast_rows(x_ref, xb_ref)
    r = jnp.dot(xb_ref[...], w_ref[...], preferred_element_type=F32)

    def residue_major(o_ref):
        dil, rows, _ = o_ref.shape
        for c in range(r_ref.shape[0]):
            cols = slice(c * LANES, (c + 1) * LANES)
            r_ref[c] = r[:, cols]
            for rho in range(dil):
                o_ref[rho, :, cols] = r_ref[c, pl.ds(rho, rows, stride=dil), :].astype(o_ref.dtype)

    @pl.when(j < n_main)
    def _():
        om_ref[0] = r.astype(om_ref.dtype)

    @pl.when((j >= n_main) & (j < n_main + n_dil))
    def _():
        residue_major(o1_ref)

    @pl.when(j >= n_main + n_dil)
    def _():
        residue_major(o2_ref)


def _projections(x, w, tm, tn):
    M, K = x.shape
    tm, tn = _pick(M, tm), _pick(3 * BRANCH_W, tn)
    n_main, n_dil = H_MAIN_COLS // tn, 3 * BRANCH_W // tn
    d1, d2 = DIL_GROUPS[1][1], DIL_GROUPS[2][1]
    assert w.shape[1] == (n_main + 2 * n_dil) * tn and tm % d2 == 0

    def out_spec(dil, first, count):
        return pl.BlockSpec((dil, tm // dil, tn), lambda i, j: (0, i, jnp.clip(j - first, 0, count - 1)))

    return pl.pallas_call(
        functools.partial(_proj_kernel, n_main=n_main, n_dil=n_dil), name="projections",
        grid=(M // tm, n_main + 2 * n_dil),
        in_specs=[pl.BlockSpec((tm, K), lambda i, j: (i, 0)), pl.BlockSpec((K, tn), lambda i, j: (0, j))],
        out_specs=[out_spec(1, 0, n_main), out_spec(d1, n_main, n_dil), out_spec(d2, n_main + n_dil, n_dil)],
        out_shape=[jax.ShapeDtypeStruct((1, M, H_MAIN_COLS), BF16),
                   jax.ShapeDtypeStruct((d1, M // d1, 3 * BRANCH_W), BF16),
                   jax.ShapeDtypeStruct((d2, M // d2, 3 * BRANCH_W), BF16)],
        scratch_shapes=[pltpu.VMEM((tm, K), BF16), pltpu.VMEM((tn // LANES, tm, LANES), F32)],
        compiler_params=_params(2),
    )(x, w)


def _swiglu_kernel(x_ref, wa_ref, wb_ref, o_ref, xb_ref):
    _cast_rows(x_ref, xb_ref)
    x = xb_ref[...]
    a = jnp.dot(x, wa_ref[...], preferred_element_type=F32)
    b = jnp.dot(x, wb_ref[...], preferred_element_type=F32)
    o_ref[...] = (a * (1.0 / (1.0 + jnp.exp(-a))) * b).astype(o_ref.dtype)


def _swiglu(x, w, tm, tn):
    M, K = x.shape
    F = w.shape[1] // 2
    tm, tn = _pick(M, tm), _pick(F, tn)
    nf = F // tn
    return pl.pallas_call(
        _swiglu_kernel, name="swiglu",
        grid=(M // tm, nf),
        in_specs=[pl.BlockSpec((tm, K), lambda i, j: (i, 0)),
                  pl.BlockSpec((K, tn), lambda i, j: (0, j)),
                  pl.BlockSpec((K, tn), lambda i, j: (0, j + nf))],
        out_specs=pl.BlockSpec((tm, tn), lambda i, j: (i, j)),
        out_shape=jax.ShapeDtypeStruct((M, F), BF16),
        scratch_shapes=[pltpu.VMEM((tm, K), BF16)],
        compiler_params=_params(2),
    )(x, w, w)


MM_LN_COL_CHUNK = 1024
MM_LN_ROW_CHUNK = 128


def _mm_res_ln_kernel(a_ref, w_ref, xres_ref, g_ref, b_ref, xo_ref, *, nk, alpha):
    k = pl.program_id(1)
    tm, D = xo_ref.shape

    cw = _pick(D, MM_LN_COL_CHUNK)

    def partial_products(accumulate):
        a = a_ref[...]
        for c in range(D // cw):
            cols = slice(c * cw, (c + 1) * cw)
            part = jnp.dot(a, w_ref[:, cols], preferred_element_type=F32)
            xo_ref[:, cols] = xo_ref[:, cols] + part if accumulate else part

    @pl.when(k == 0)
    def _():
        partial_products(False)

    @pl.when(k > 0)
    def _():
        partial_products(True)

    @pl.when(k == nk - 1)
    def _():
        rw = _pick(tm, MM_LN_ROW_CHUNK)
        for r in range(tm // rw):
            rows = slice(r * rw, (r + 1) * rw)
            xo_ref[rows, :] = _ln_rows(alpha * xres_ref[rows, :] + xo_ref[rows, :], g_ref[...], b_ref[...])


def _mm_res_ln(a, w, xres, g, b, alpha, tm, tk, name):
    M, K = a.shape
    D = w.shape[1]
    tm, tk = _pick(M, tm), _pick(K, tk)
    nk = K // tk
    row = pl.BlockSpec((tm, D), lambda i, k: (i, 0))
    vec = pl.BlockSpec((1, D), lambda i, k: (0, 0))
    return pl.pallas_call(
        functools.partial(_mm_res_ln_kernel, nk=nk, alpha=alpha), name=name,
        grid=(M // tm, nk),
        in_specs=[pl.BlockSpec((tm, tk), lambda i, k: (i, k)),
                  pl.BlockSpec((tk, D), lambda i, k: (k, 0)),
                  row, vec, vec],
        out_specs=row,
        out_shape=jax.ShapeDtypeStruct((M, D), F32),
        compiler_params=_params(2),
    )(a, w, xres, g.reshape(1, D), b.reshape(1, D))


def _gate_merge_kernel(gz_ref, oa_ref, ob_ref, oc_ref, od_ref, wup_ref, bg_ref, wbr_ref, o_ref):
    gz = gz_ref[...]
    acc = None
    for n, o_n in enumerate((oa_ref, ob_ref, oc_ref, od_ref)):
        z = jnp.dot(gz, wup_ref[n], preferred_element_type=F32) + bg_ref[n:n + 1, :]
        t = jnp.dot(o_n[...], wbr_ref[n], preferred_element_type=F32)
        term = t * (1.0 / (1.0 + jnp.exp(-z)))
        acc = term if acc is None else acc + term
    o_ref[...] = acc.astype(o_ref.dtype)


def _gate_merge(h, o_a, o_b, o_c, o_d, wup, bg, wbr, tm, tn):
    M = h.shape[0]
    D = wup.shape[2]
    tm, tn = _pick(M, tm), _pick(D, tn)
    o_spec = pl.BlockSpec((tm, BRANCH_W), lambda i, j: (i, 0))
    return pl.pallas_call(
        _gate_merge_kernel, name="gate_merge",
        grid=(M // tm, D // tn),
        in_specs=[pl.BlockSpec((tm, GATE_RANK), lambda i, j: (i, COL_GZ // GATE_RANK)),
                  o_spec, o_spec, o_spec, o_spec,
                  pl.BlockSpec((N_BRANCH, GATE_RANK, tn), lambda i, j: (0, 0, j)),
                  pl.BlockSpec((N_BRANCH, tn), lambda i, j: (0, j)),
                  pl.BlockSpec((N_BRANCH, BRANCH_W, tn), lambda i, j: (0, 0, j))],
        out_specs=pl.BlockSpec((tm, tn), lambda i, j: (i, j)),
        out_shape=jax.ShapeDtypeStruct((M, D), BF16),
        compiler_params=_params(2),
    )(h, o_a, o_b, o_c, o_d, wup, bg, wbr)


def _lane():
    return lax.broadcasted_iota(jnp.int32, (1, LANES), 1)


def _dot_nt(a, b):
    return lax.dot_general(a, b, (((1,), (1,)), ((), ())), preferred_element_type=F32)


def _split_heads(x):
    lo = (_lane() < HEAD_DIM).astype(F32)
    return jnp.concatenate([(x * lo).astype(BF16), (x * (1.0 - lo)).astype(BF16)], axis=0)


def _merge_heads(o, rows):
    return jnp.where(_lane() < HEAD_DIM, o[:rows], o[rows:])


def _group_mean(x):
    r = lax.broadcasted_iota(jnp.int32, (LANES, LANES), 0) >> 6
    c = lax.broadcasted_iota(jnp.int32, (LANES, LANES), 1) >> 6
    p = jnp.where(r == c, 1.0 / HEAD_DIM, 0.0).astype(BF16)
    hi = x.astype(BF16)
    lo = (x - hi.astype(F32)).astype(BF16)
    return jnp.dot(hi, p, preferred_element_type=F32) + jnp.dot(lo, p, preferred_element_type=F32)


def _rms_heads(x, g):
    return x * lax.rsqrt(_group_mean(x * x) + RMS_EPS) * g


FLASH_ROWS = 32
LOG2E = math.log2(math.e)


def _flash_scratch(rows, tk):
    wide = lambda dt: pltpu.VMEM((rows, tk), dt)
    narrow = pltpu.VMEM((rows, LANES), F32)
    return [wide(F32), wide(F32), wide(BF16), wide(BF16)] + [narrow] * 5


def _flash(qs, k_ref, v_ref, scratch, n_chunks, tk, bias_fn=None):
    s_refs, p_refs, a_refs = scratch[0:2], scratch[2:4], scratch[4:6]
    m_ref, l_ref, acc_ref = scratch[6:9]
    rows = qs.shape[0]
    rb = FLASH_ROWS
    assert rows % rb == 0
    m_ref[...] = jnp.full(m_ref.shape, NEG_INF, F32)
    l_ref[...] = jnp.zeros(l_ref.shape, F32)
    acc_ref[...] = jnp.zeros(acc_ref.shape, F32)
    p_refs[1][...] = jnp.zeros(p_refs[1].shape, BF16)
    a_refs[1][...] = jnp.ones(a_refs[1].shape, F32)

    def qk(t, dst):
        dst[...] = _dot_nt(qs, k_ref[pl.ds(pl.multiple_of(t * tk, tk), tk), :])

    def pv(t, par):
        vb = v_ref[pl.ds(pl.multiple_of(t * tk, tk), tk), :]
        acc_ref[...] = a_refs[par][...] * acc_ref[...] + jnp.dot(p_refs[par][...], vb, preferred_element_type=F32)

    def softmax(t, par):
        s_ref, p_ref, a_ref = s_refs[par], p_refs[par], a_refs[par]
        ncol = tk // LANES
        for i in range(rows // rb):
            rs = slice(i * rb, (i + 1) * rb)
            s = [s_ref[rs, c * LANES:(c + 1) * LANES] for c in range(ncol)]
            if bias_fn is not None:
                s = [sc + bias_fn(t, i * rb, c) for c, sc in enumerate(s)]
                for c in range(ncol):
                    s_ref[rs, c * LANES:(c + 1) * LANES] = s[c]
            m_old = m_ref[rs, :]
            mx = functools.reduce(jnp.maximum, s)
            m_new = jnp.maximum(m_old, jnp.broadcast_to(jnp.max(mx, axis=-1, keepdims=True), (rb, LANES)))
            m_ref[rs, :] = m_new
            a_ref[rs, :] = jnp.exp2(m_old - m_new)
        for i in range(rows // rb):
            rs = slice(i * rb, (i + 1) * rb)
            m_new = m_ref[rs, :]
            p = [jnp.exp2(s_ref[rs, c * LANES:(c + 1) * LANES] - m_new) for c in range(ncol)]
            l_ref[rs, :] = a_ref[rs, :] * l_ref[rs, :] + functools.reduce(jnp.add, p)
            for c in range(ncol):
                p_ref[rs, c * LANES:(c + 1) * LANES] = p[c].astype(BF16)

    def step(t, par):
        qk(jnp.minimum(t + 1, n_chunks - 1), s_refs[1 - par])
        pv(jnp.maximum(t - 1, 0), 1 - par)
        softmax(t, par)

    def body(t, carry):
        @pl.when(t % 2 == 0)
        def _():
            step(t, 0)

        @pl.when(t % 2 == 1)
        def _():
            step(t, 1)
        return carry

    qk(0, s_refs[0])
    lax.fori_loop(0, n_chunks, body, 0)
    pv(n_chunks - 1, (n_chunks - 1) % 2)
    return acc_ref[...] / jnp.sum(l_ref[...], axis=-1, keepdims=True)


def _t5_bucket(rel):
    half = T5_BUCKETS // 2
    max_exact = half // 2
    n = jnp.abs(rel)
    nf = jnp.maximum(n, 1).astype(jnp.float32)
    large = max_exact + (jnp.log(nf / max_exact) / math.log(T5_MAX_DIST / max_exact)
                         * (half - max_exact)).astype(jnp.int32)
    large = jnp.minimum(large, half - 1)
    return jnp.where(rel > 0, half, 0) + jnp.where(n < max_exact, n, large)


def _toeplitz(x, rows, cols):
    n = rows + cols
    pad = n - x.shape[-1]
    x = x[..., :n] if pad <= 0 else jnp.pad(x, [(0, 0)] * (x.ndim - 1) + [(0, pad)])
    t = jnp.tile(x, (1,) * (x.ndim - 1) + (rows,))[..., :rows * (n - 1)]
    return t.reshape(x.shape[:-1] + (rows, n - 1))[..., rows - 1:rows - 1 + cols]


def _na_bias_table(rpb):
    H = rpb.shape[0]
    qcol = jnp.arange(GRID_W, dtype=jnp.int32)
    kcol = jnp.arange(GRID_W, dtype=jnp.int32)
    c0 = jnp.clip(qcol - NA_COLS // 2, 0, GRID_W - NA_COLS)
    col_ok = (kcol[None, :] >= c0[:, None]) & (kcol[None, :] < c0[:, None] + NA_COLS)
    rel = jnp.arange(-(GRID_W - 1), GRID_W, dtype=jnp.int32)
    by_rel = rpb[:, :, jnp.clip(rel + NA_COLS - 1, 0, 2 * NA_COLS - 2)]
    rpb_col = jnp.where(col_ok, _toeplitz(by_rel, GRID_W, GRID_W), NEG_INF)
    t = jnp.stack([rpb_col[:, NA_ROWS - 1 - d:2 * NA_ROWS - 1 - d] for d in range(NA_ROWS)])
    return t.transpose(0, 1, 3, 2, 4).reshape(NA_ROWS, H, GRID_W, NA_ROWS * GRID_W).astype(F32)


def _na_kernel(q_ref, k_ref, v_ref, bias_ref, o_ref, *, R):
    r = pl.program_id(1)
    r0 = jnp.clip(r - NA_ROWS // 2, 0, R - NA_ROWS)
    delta = r - r0
    ks = pl.multiple_of(r0 * GRID_W, GRID_W)
    nk = NA_ROWS * GRID_W
    for c in range(BRANCH_W // LANES):
        cols = slice(c * LANES, (c + 1) * LANES)
        qs = _split_heads(q_ref[:, cols].astype(F32) * HEAD_DIM ** -0.5)
        s = _dot_nt(qs, k_ref[pl.ds(ks, nk), cols])
        s = s + jnp.concatenate([bias_ref[delta, 2 * c], bias_ref[delta, 2 * c + 1]], axis=0)
        m = jnp.max(s, axis=-1, keepdims=True)
        p = jnp.exp(s - m)
        l = jnp.sum(p, axis=-1, keepdims=True)
        o = jnp.dot(p.astype(BF16), v_ref[pl.ds(ks, nk), cols], preferred_element_type=F32) / l
        o_ref[:, cols] = _merge_heads(o, GRID_W).astype(o_ref.dtype)


def _na_attention(h, bias, B, T):
    R = T // GRID_W
    assert R >= NA_ROWS
    M = h.shape[0]
    return pl.pallas_call(
        functools.partial(_na_kernel, R=R), name="na_attn",
        grid=(B, R),
        in_specs=[pl.BlockSpec((GRID_W, BRANCH_W), lambda b, r: (b * R + r, COL_QA // BRANCH_W)),
                  pl.BlockSpec((T, BRANCH_W), lambda b, r: (b, COL_KA // BRANCH_W)),
                  pl.BlockSpec((T, BRANCH_W), lambda b, r: (b, COL_VA // BRANCH_W)),
                  pl.BlockSpec(bias.shape, lambda b, r: (0, 0, 0, 0))],
        out_specs=pl.BlockSpec((GRID_W, BRANCH_W), lambda b, r: (b * R + r, 0)),
        out_shape=jax.ShapeDtypeStruct((M, BRANCH_W), BF16),
        compiler_params=_params(2),
    )(h, h, h, bias)


DIFF_TQ, DIFF_TK = 256, 512
DIFF_D_LO = -(DIFF_TK + DIFF_TQ)
DIFF_D_HI = 2 * DIFF_TQ
assert DIFF_TQ >= T5_MAX_DIST and DIFF_TK % DIFF_TQ == 0


def _diff_bias_tiles(t5_tab):
    tq, tk = DIFF_TQ, DIFF_TK
    lo = DIFF_D_LO - (tq - 1)
    rel = jnp.arange(lo, DIFF_D_HI + tk, dtype=jnp.int32)
    by_rel = t5_tab[_t5_bucket(rel)].T
    offs = range(DIFF_D_LO, DIFF_D_HI + 1, tq)
    x = jnp.stack([by_rel[:, d - (tq - 1) - lo:d + tk - lo] for d in offs], axis=1)
    return (_toeplitz(x, tq, tk) * LOG2E).astype(F32)


def _diff_kernel(lam_ref, q_ref, k_ref, v_ref, tiles_ref, g_ref, o_ref, *scratch, n_chunks, out_scale):
    qi = pl.program_id(2)
    tq, tk = DIFF_TQ, DIFF_TK
    lane = _lane()
    qf = q_ref[...].astype(F32) * (DIFF_D ** -0.5 * LOG2E)
    qs = jnp.concatenate([(qf * ((lane >> 5) == v).astype(F32)).astype(BF16) for v in range(4)], axis=0)

    def bias_fn(kc, r0, c):
        d = jnp.clip(kc * tk - qi * tq, DIFF_D_LO, DIFF_D_HI)
        j = (d - DIFF_D_LO) // tq
        return tiles_ref[r0 // (2 * tq), j, r0 % tq:r0 % tq + FLASH_ROWS, c * LANES:(c + 1) * LANES]

    o = _flash(qs, k_ref, v_ref, scratch, n_chunks, tk, bias_fn)
    lam = lam_ref[0]
    x = jnp.where(lane < HEAD_DIM, o[:tq] - lam * o[tq:2 * tq], o[2 * tq:3 * tq] - lam * o[3 * tq:])
    o_ref[...] = (_rms_heads(x, g_ref[...]) * out_scale).astype(o_ref.dtype)


def _diff_attention(h, tiles, lam, g, lam_init, B, T):
    M = h.shape[0]
    tq, tk = DIFF_TQ, DIFF_TK
    assert T % tk == 0
    nq = T // tq
    qb, kb, vb = COL_QB // LANES, COL_KB // LANES, COL_VB // LANES
    g2 = jnp.tile(g.astype(F32), 2).reshape(1, LANES)
    return pl.pallas_call(
        functools.partial(_diff_kernel, n_chunks=T // tk, out_scale=1.0 - lam_init), name="diff_attn",
        grid=(B, DIFF_HEADS // 2, nq),
        in_specs=[pl.BlockSpec(memory_space=pltpu.SMEM),
                  pl.BlockSpec((tq, LANES), lambda b, hp, qi: (b * nq + qi, qb + hp)),
                  pl.BlockSpec((T, LANES), lambda b, hp, qi: (b, kb + hp)),
                  pl.BlockSpec((T, LANES), lambda b, hp, qi: (b, vb + hp)),
                  pl.BlockSpec((2,) + tiles.shape[1:], lambda b, hp, qi: (hp, 0, 0, 0)),
                  pl.BlockSpec((1, LANES), lambda b, hp, qi: (0, 0))],
        out_specs=pl.BlockSpec((tq, LANES), lambda b, hp, qi: (b * nq + qi, hp)),
        out_shape=jax.ShapeDtypeStruct((M, BRANCH_W), BF16),
        scratch_shapes=_flash_scratch(4 * tq, tk),
        compiler_params=_params(3),
    )(lam.reshape(1).astype(F32), h, h, h, tiles, g2)


GQA_TQ, GQA_TK = 128, 512


def _rope_tables(T):
    t = jnp.arange(T, dtype=jnp.int32)
    row = (t // GRID_W).astype(jnp.float32)
    col = (t % GRID_W).astype(jnp.float32)
    freqs = ROPE_THETA ** (-jnp.arange(0, ROPE_AXIS_DIM, 2, dtype=jnp.float32) / ROPE_AXIS_DIM)
    ang = jnp.concatenate([row[:, None] * freqs, col[:, None] * freqs], axis=-1)
    cos, sin = jnp.cos(ang), jnp.sin(ang)
    cos_t = jnp.tile(cos, (1, 4))
    sin_t = jnp.tile(jnp.concatenate([-sin, sin], axis=-1), (1, 2))
    return cos_t.astype(F32), sin_t.astype(F32)


def _rope(x, cos, sin):
    first = (_lane() & (HEAD_DIM - 1)) < HEAD_DIM // 2
    partner = jnp.where(first, pltpu.roll(x, LANES - HEAD_DIM // 2, axis=1), pltpu.roll(x, HEAD_DIM // 2, axis=1))
    return x * cos + partner * sin


def _gqa_kernel(q_ref, k_ref, v_ref, cq_ref, sq_ref, ck_ref, sk_ref, gq_ref, gk_ref, o_ref, kproc_ref, *scratch,
                n_chunks):
    qi = pl.program_id(1)
    tq = GQA_TQ

    @pl.when(qi == 0)
    def _():
        kf = _rms_heads(k_ref[...].astype(F32), gk_ref[...])
        kproc_ref[...] = _rope(kf, ck_ref[...], sk_ref[...]).astype(BF16)

    cq, sq, gq = cq_ref[...], sq_ref[...], gq_ref[...]
    parts = []
    for c in range(BRANCH_W // LANES):
        qf = _rms_heads(q_ref[:, c * LANES:(c + 1) * LANES].astype(F32), gq)
        parts.append(_split_heads(_rope(qf, cq, sq) * (HEAD_DIM ** -0.5 * LOG2E)))
    o = _flash(jnp.concatenate(parts, axis=0), kproc_ref, v_ref, scratch, n_chunks, GQA_TK)
    for c in range(BRANCH_W // LANES):
        o_ref[:, c * LANES:(c + 1) * LANES] = _merge_heads(o[2 * c * tq:(2 * c + 2) * tq], tq).astype(o_ref.dtype)


def _gqa_attention(h, cos_t, sin_t, gq, gk, B, T):
    M = h.shape[0]
    tq, tk = GQA_TQ, GQA_TK
    assert T % tk == 0
    nq = T // tq
    vec = pl.BlockSpec((1, LANES), lambda b, qi: (0, 0))
    qtab = pl.BlockSpec((tq, LANES), lambda b, qi: (qi, 0))
    ktab = pl.BlockSpec((T, LANES), lambda b, qi: (0, 0))
    return pl.pallas_call(
        functools.partial(_gqa_kernel, n_chunks=T // tk), name="gqa_attn",
        grid=(B, nq),
        in_specs=[pl.BlockSpec((tq, BRANCH_W), lambda b, qi: (b * nq + qi, COL_QC // BRANCH_W)),
                  pl.BlockSpec((T, LANES), lambda b, qi: (b, COL_KC // LANES)),
                  pl.BlockSpec((T, LANES), lambda b, qi: (b, COL_VC // LANES)),
                  qtab, qtab, ktab, ktab, vec, vec],
        out_specs=pl.BlockSpec((tq, BRANCH_W), lambda b, qi: (b * nq + qi, 0)),
        out_shape=jax.ShapeDtypeStruct((M, BRANCH_W), BF16),
        scratch_shapes=[pltpu.VMEM((T, LANES), BF16)] + _flash_scratch(GQA_HEADS * tq, tk),
        compiler_params=_params(2),
    )(h, h, h, cos_t, sin_t, cos_t, sin_t, gq, gk)


DIL_TQ = 128
DIL_HALF = 64
assert all(w // (2 * d) == DIL_HALF for w, d in DIL_GROUPS)


def _dil_bias_tiles(t5_tab, g, L):
    dil = DIL_GROUPS[g][1]
    tq = DIL_TQ
    W = min(2 * tq, L)
    offs = range(0, -(W - tq) - 1, -DIL_HALF)
    lo = offs[-1] - (tq - 1)
    rel = jnp.arange(lo, W, dtype=jnp.int32)
    s0 = DIFF_HEADS + g * DIL_SLOTS
    by_rel = t5_tab[:, s0:s0 + DIL_SLOTS][_t5_bucket(rel * dil)]
    by_rel = jnp.where((jnp.abs(rel) <= DIL_HALF)[:, None], by_rel, NEG_INF).T
    x = jnp.stack([by_rel[:, d - (tq - 1) - lo:d + W - lo] for d in offs])
    return _toeplitz(x, tq, W).astype(F32)


def _dil_kernel(q_ref, k_ref, v_ref, tiles_ref, o_ref, lse_ref, *, L, W):
    n = pl.program_id(2)
    tq = DIL_TQ
    ws = jnp.clip(n * tq - DIL_HALF, 0, L - W)
    var = (n * tq - ws) // DIL_HALF
    ws = pl.multiple_of(ws, DIL_HALF)
    for c in range(BRANCH_W // LANES):
        cols = slice(c * LANES, (c + 1) * LANES)
        qs = _split_heads(q_ref[:, cols].astype(F32) * HEAD_DIM ** -0.5)
        s = _dot_nt(qs, k_ref[pl.ds(ws, W), cols])
        s = s + jnp.concatenate([tiles_ref[var, 2 * c], tiles_ref[var, 2 * c + 1]], axis=0)
        m = jnp.max(s, axis=-1, keepdims=True)
        p = jnp.exp(s - m)
        l = jnp.sum(p, axis=-1, keepdims=True)
        o = jnp.dot(p.astype(BF16), v_ref[pl.ds(ws, W), cols], preferred_element_type=F32) / l
        lse = jnp.broadcast_to(m + jnp.log(l), (2 * tq, LANES))
        o_ref[:, cols] = _merge_heads(o, tq).astype(o_ref.dtype)
        lse_ref[:, cols] = _merge_heads(lse, tq)


def _dil_attention(hd, col0, tiles, B, T):
    dil, Md, _ = hd.shape
    L = T // dil
    tq = DIL_TQ
    W = min(2 * tq, L)
    assert L % tq == 0
    nq = L // tq
    out_spec = pl.BlockSpec((None, tq, BRANCH_W), lambda b, rho, n: (rho, b * nq + n, 0))
    return pl.pallas_call(
        functools.partial(_dil_kernel, L=L, W=W), name=f"dil_attn_{dil}",
        grid=(B, dil, nq),
        in_specs=[pl.BlockSpec((None, tq, BRANCH_W), lambda b, rho, n: (rho, b * nq + n, col0)),
                  pl.BlockSpec((None, L, BRANCH_W), lambda b, rho, n: (rho, b, col0 + 1)),
                  pl.BlockSpec((None, L, BRANCH_W), lambda b, rho, n: (rho, b, col0 + 2)),
                  pl.BlockSpec(tiles.shape, lambda b, rho, n: (0, 0, 0, 0))],
        out_specs=[out_spec, out_spec],
        out_shape=[jax.ShapeDtypeStruct((dil, Md, BRANCH_W), BF16), jax.ShapeDtypeStruct((dil, Md, BRANCH_W), F32)],
        compiler_params=_params(3),
    )(hd, hd, hd, tiles)


def _dil_merge_kernel(o0_ref, o1_ref, o2_ref, l0_ref, l1_ref, l2_ref, o_ref, *scratch):
    def natural(ref, scr):
        dil, rows, _ = ref.shape
        for c in range(scr.shape[0]):
            for rho in range(dil):
                scr[c, pl.ds(rho, rows, stride=dil), :] = ref[rho, :, c * LANES:(c + 1) * LANES].astype(F32)
        return jnp.concatenate([scr[c] for c in range(scr.shape[0])], axis=1)

    os_ = [o0_ref[0].astype(F32), natural(o1_ref, scratch[0]), natural(o2_ref, scratch[1])]
    ls = [l0_ref[0], natural(l1_ref, scratch[2]), natural(l2_ref, scratch[3])]
    mx = jnp.maximum(jnp.maximum(ls[0], ls[1]), ls[2])
    es = [jnp.exp(l - mx) for l in ls]
    den = es[0] + es[1] + es[2]
    num = es[0] * os_[0] + es[1] * os_[1] + es[2] * os_[2]
    o_ref[...] = (num / den).astype(o_ref.dtype)


def _dil_merge(outs, lses):
    M = outs[0].shape[1]
    tm = _pick(M, 512)

    def spec(a):
        dil = a.shape[0]
        return pl.BlockSpec((dil, tm // dil, BRANCH_W), lambda i: (0, i, 0))

    return pl.pallas_call(
        _dil_merge_kernel, name="dil_merge",
        grid=(M // tm,),
        in_specs=[spec(a) for a in (*outs, *lses)],
        out_specs=pl.BlockSpec((tm, BRANCH_W), lambda i: (i, 0)),
        out_shape=jax.ShapeDtypeStruct((M, BRANCH_W), BF16),
        scratch_shapes=[pltpu.VMEM((BRANCH_W // LANES, tm, LANES), F32)] * 4,
        compiler_params=_params(1),
    )(*outs, *lses)


def _deinterleave(w, n_heads):
    d = w.shape[0]
    return w.reshape(d, n_heads, HEAD_DIM // 2, 2).transpose(0, 1, 3, 2).reshape(d, n_heads * HEAD_DIM)


def _prep_layer(l, w_in, w_gate_down, w_gate_up, w_branch, w_out, w_ffn_in, w_ffn_out, ff_pad):
    d = w_in.shape[1]
    wi = w_in[l]
    qc = _deinterleave(wi[:, 3072:3584], GQA_HEADS).reshape(d, GQA_HEADS, HEAD_DIM)
    qc = qc[:, jnp.array(GQA_HEAD_ORDER)].reshape(d, GQA_HEADS * HEAD_DIM)
    kc = _deinterleave(wi[:, 3584:3712], GQA_KV_HEADS)
    qd, kd, vd = wi[:, 3840:5376], wi[:, 5376:6912], wi[:, 6912:8448]

    def group(g):
        s = slice(g * BRANCH_W, (g + 1) * BRANCH_W)
        return [qd[:, s], kd[:, s], vd[:, s]]

    w_proj = jnp.concatenate([wi[:, :3072], qc, kc, wi[:, 3712:3840], w_gate_down[l]]
                             + group(0) + group(1) + group(2), axis=1)
    assert w_proj.shape[1] == H_MAIN_COLS + 6 * BRANCH_W
    wup = w_gate_up[l].reshape(GATE_RANK, N_BRANCH, d).transpose(1, 0, 2)
    wbr = w_branch[l]
    wbr_c = wbr[2].reshape(GQA_HEADS, HEAD_DIM, d)[jnp.array(GQA_HEAD_ORDER)].reshape(BRANCH_W, d)
    wbr = jnp.stack([wbr[0], wbr[1], wbr_c, wbr[3]])
    ff = w_ffn_out.shape[1]
    wfi = w_ffn_in[l]
    pad = ff_pad - ff
    wfi = jnp.concatenate([jnp.pad(wfi[:, :ff], ((0, 0), (0, pad))), jnp.pad(wfi[:, ff:], ((0, 0), (0, pad)))], axis=1)
    wfo = jnp.pad(w_ffn_out[l], ((0, pad), (0, 0)))
    return dict(w_proj=w_proj.astype(BF16), wup=wup.astype(BF16), wbr=wbr.astype(BF16),
                w_out=w_out[l].astype(BF16), wfi=wfi.astype(BF16), wfo=wfo.astype(BF16))


def _trunk(x, consts, layers, p):
    B, T, D = x.shape
    M = B * T
    depth = len(layers)
    alpha = (2 * depth) ** 0.25
    cos_t, sin_t = _rope_tables(T)
    dil_tiles = [_dil_bias_tiles(p['t5_table'], g, T // dil) for g, (_, dil) in enumerate(DIL_GROUPS)]
    xf = _ln_in(x.reshape(M, D), p['ln_in_g'], p['ln_in_b'])
    for l, lw in enumerate(layers):
        lam_init = 0.8 - 0.6 * math.exp(-0.3 * l)
        h3, hd1, hd2 = _projections(xf, lw['w_proj'], 512, 512)
        h = h3[0]
        o_a = _na_attention(h, consts['na_bias'][l], B, T)
        lq = p['diff_lambda'][l].astype(F32)
        lam = jnp.exp(jnp.sum(lq[0] * lq[1])) - jnp.exp(jnp.sum(lq[2] * lq[3])) + lam_init
        o_b = _diff_attention(h, consts['diff_tiles'], lam, p['diff_subln_g'][l], lam_init, B, T)
        gq = jnp.tile(jnp.concatenate([p['qk_norm_g'][l, 0, 0::2], p['qk_norm_g'][l, 0, 1::2]]), 2).reshape(1, LANES)
        gk = jnp.tile(jnp.concatenate([p['qk_norm_g'][l, 1, 0::2], p['qk_norm_g'][l, 1, 1::2]]), 2).reshape(1, LANES)
        o_c = _gqa_attention(h, cos_t, sin_t, gq.astype(F32), gk.astype(F32), B, T)
        dil_out = [_dil_attention(h3, COL_D0 // BRANCH_W, dil_tiles[0], B, T),
                   _dil_attention(hd1, 0, dil_tiles[1], B, T),
                   _dil_attention(hd2, 0, dil_tiles[2], B, T)]
        o_d = _dil_merge([o for o, _ in dil_out], [s for _, s in dil_out])
        merged = _gate_merge(h, o_a, o_b, o_c, o_d, lw['wup'], p['b_gate'][l].astype(F32), lw['wbr'], 512, 1024)
        xf = _mm_res_ln(merged, lw['w_out'], xf, p['ln1_g'][l], p['ln1_b'][l], alpha, 512, 512, "out_proj_ln")
        ff = _swiglu(xf, lw['wfi'], 1024, 256)
        xf = _mm_res_ln(ff, lw['wfo'], xf, p['ln2_g'][l], p['ln2_b'][l], alpha, 512, 512, "ffn_out_ln")
    return xf.reshape(B, T, D)


def kernel(x_prompt, x_sample, ln_in_g, ln_in_b, w_in, na_rpb, qk_norm_g, diff_lambda, diff_subln_g, t5_table,
           w_gate_down, w_gate_up, b_gate, w_branch, w_out, ln1_g, ln1_b, w_ffn_in, w_ffn_out, ln2_g, ln2_b):
    depth = w_in.shape[0]
    ff = w_ffn_out.shape[1]
    ff_pad = -(-ff // 1024) * 1024
    layers = [_prep_layer(l, w_in, w_gate_down, w_gate_up, w_branch, w_out, w_ffn_in, w_ffn_out, ff_pad)
              for l in range(depth)]
    consts = dict(na_bias=[_na_bias_table(na_rpb[l]) for l in range(depth)],
                  diff_tiles=_diff_bias_tiles(t5_table[:, :DIFF_HEADS]))
    p = dict(ln_in_g=ln_in_g, ln_in_b=ln_in_b, qk_norm_g=qk_norm_g, diff_lambda=diff_lambda,
             diff_subln_g=diff_subln_g, t5_table=t5_table, b_gate=b_gate, ln1_g=ln1_g, ln1_b=ln1_b,
             ln2_g=ln2_g, ln2_b=ln2_b)
    return (_trunk(x_prompt, consts, layers, p), _trunk(x_sample, consts, layers, p))
```

```python
import functools
import math

import jax
import jax.numpy as jnp
from jax import lax
from jax.experimental import pallas as pl
from jax.experimental.pallas import tpu as pltpu

F32 = jnp.float32
BF16 = jnp.bfloat16

HEAD_DIM = 64
LANES = 128
GRID_W = 64
NA_ROWS = 8
NA_COLS = 16
DIFF_HEADS = 8
DIFF_D = HEAD_DIM // 2
GQA_HEADS = 8
GQA_KV_HEADS = 2
ROPE_AXIS_DIM = HEAD_DIM // 2
ROPE_THETA = 10000.0
DIL_SLOTS = 8
DIL_GROUPS = ((128, 1), (512, 4), (2048, 16))
T5_BUCKETS = 32
T5_MAX_DIST = 128
N_BRANCH = 4
BRANCH_W = 8 * HEAD_DIM
GATE_RANK = 256
LN_EPS = 1e-5
RMS_EPS = 1e-6
NEG_INF = -1e30

COL_QA, COL_KA, COL_VA = 0, 512, 1024
COL_QB, COL_KB, COL_VB = 1536, 2048, 2560
COL_QC = 3072
COL_KC, COL_VC = 3584, 3712
COL_GZ = 3840
COL_D0 = 4096
H_MAIN_COLS = COL_D0 + 3 * BRANCH_W
GQA_HEAD_ORDER = (0, 4, 1, 5, 2, 6, 3, 7)

VMEM_LIMIT = 56 * 1024 * 1024


def _params(n_axes, vmem=VMEM_LIMIT):
    return pltpu.CompilerParams(dimension_semantics=("arbitrary",) * n_axes, vmem_limit_bytes=vmem)


def _pick(n, pref):
    t = min(n, pref)
    while n % t:
        t //= 2
    return t


def _ln_rows(y, g, b):
    mu = jnp.mean(y, axis=-1, keepdims=True)
    yc = y - mu
    var = jnp.mean(yc * yc, axis=-1, keepdims=True)
    return yc * lax.rsqrt(var + LN_EPS) * g + b


def _ln_in_kernel(x_ref, g_ref, b_ref, xo_ref):
    xo_ref[...] = _ln_rows(x_ref[...], g_ref[...], b_ref[...])


def _ln_in(x, g, b):
    M, D = x.shape
    tm = _pick(M, 256)
    row = pl.BlockSpec((tm, D), lambda i: (i, 0))
    vec = pl.BlockSpec((1, D), lambda i: (0, 0))
    return pl.pallas_call(
        _ln_in_kernel, name="ln_in",
        grid=(M // tm,),
        in_specs=[row, vec, vec],
        out_specs=row,
        out_shape=jax.ShapeDtypeStruct((M, D), F32),
        compiler_params=_params(1),
    )(x, g.reshape(1, D), b.reshape(1, D))


def _cast_rows(x_ref, xb_ref):
    @pl.when(pl.program_id(1) == 0)
    def _():
        xb_ref[...] = x_ref[...].astype(BF16)


def _proj_kernel(x_ref, w_ref, om_ref, o1_ref, o2_ref, xb_ref, r_ref, *, n_main, n_dil):
    j = pl.program_id(1)
    _cast_rows(x_ref, xb_ref)
    r = jnp.dot(xb_ref[...], w_ref[...], preferred_element_type=F32)

    def residue_major(o_ref):
        dil, rows, _ = o_ref.shape
        for c in range(r_ref.shape[0]):
            cols = slice(c * LANES, (c + 1) * LANES)
            r_ref[c] = r[:, cols]
            for rho in range(dil):
                o_ref[rho, :, cols] = r_ref[c, pl.ds(rho, rows, stride=dil), :].astype(o_ref.dtype)

    @pl.when(j < n_main)
    def _():
        om_ref[0] = r.astype(om_ref.dtype)

    @pl.when((j >= n_main) & (j < n_main + n_dil))
    def _():
        residue_major(o1_ref)

    @pl.when(j >= n_main + n_dil)
    def _():
        residue_major(o2_ref)


def _projections(x, w, tm, tn):
    M, K = x.shape
    tm, tn = _pick(M, tm), _pick(3 * BRANCH_W, tn)
    n_main, n_dil = H_MAIN_COLS // tn, 3 * BRANCH_W // tn
    d1, d2 = DIL_GROUPS[1][1], DIL_GROUPS[2][1]
    assert w.shape[1] == (n_main + 2 * n_dil) * tn and tm % d2 == 0

    def out_spec(dil, first, count):
        return pl.BlockSpec((dil, tm // dil, tn), lambda i, j: (0, i, jnp.clip(j - first, 0, count - 1)))

    return pl.pallas_call(
        functools.partial(_proj_kernel, n_main=n_main, n_dil=n_dil), name="projections",
        grid=(M // tm, n_main + 2 * n_dil),
        in_specs=[pl.BlockSpec((tm, K), lambda i, j: (i, 0)), pl.BlockSpec((K, tn), lambda i, j: (0, j))],
        out_specs=[out_spec(1, 0, n_main), out_spec(d1, n_main, n_dil), out_spec(d2, n_main + n_dil, n_dil)],
        out_shape=[jax.ShapeDtypeStruct((1, M, H_MAIN_COLS), BF16),
                   jax.ShapeDtypeStruct((d1, M // d1, 3 * BRANCH_W), BF16),
                   jax.ShapeDtypeStruct((d2, M // d2, 3 * BRANCH_W), BF16)],
        scratch_shapes=[pltpu.VMEM((tm, K), BF16), pltpu.VMEM((tn // LANES, tm, LANES), F32)],
        compiler_params=_params(2),
    )(x, w)


def _swiglu_kernel(x_ref, wa_ref, wb_ref, o_ref, xb_ref):
    _cast_rows(x_ref, xb_ref)
    x = xb_ref[...]
    a = jnp.dot(x, wa_ref[...], preferred_element_type=F32)
    b = jnp.dot(x, wb_ref[...], preferred_element_type=F32)
    o_ref[...] = (a * (1.0 / (1.0 + jnp.exp(-a))) * b).astype(o_ref.dtype)


def _swiglu(x, w, tm, tn):
    M, K = x.shape
    F = w.shape[1] // 2
    tm, tn = _pick(M, tm), _pick(F, tn)
    nf = F // tn
    return pl.pallas_call(
        _swiglu_kernel, name="swiglu",
        grid=(M // tm, nf),
        in_specs=[pl.BlockSpec((tm, K), lambda i, j: (i, 0)),
                  pl.BlockSpec((K, tn), lambda i, j: (0, j)),
                  pl.BlockSpec((K, tn), lambda i, j: (0, j + nf))],
        out_specs=pl.BlockSpec((tm, tn), lambda i, j: (i, j)),
        out_shape=jax.ShapeDtypeStruct((M, F), BF16),
        scratch_shapes=[pltpu.VMEM((tm, K), BF16)],
        compiler_params=_params(2),
    )(x, w, w)


MM_LN_COL_CHUNK = 1024
MM_LN_ROW_CHUNK = 128


def _mm_res_ln_kernel(a_ref, w_ref, xres_ref, g_ref, b_ref, xo_ref, *, nk, alpha):
    k = pl.program_id(1)
    tm, D = xo_ref.shape

    cw = _pick(D, MM_LN_COL_CHUNK)

    def partial_products(accumulate):
        a = a_ref[...]
        for c in range(D // cw):
            cols = slice(c * cw, (c + 1) * cw)
            part = jnp.dot(a, w_ref[:, cols], preferred_element_type=F32)
            xo_ref[:, cols] = xo_ref[:, cols] + part if accumulate else part

    @pl.when(k == 0)
    def _():
        partial_products(False)

    @pl.when(k > 0)
    def _():
        partial_products(True)

    @pl.when(k == nk - 1)
    def _():
        rw = _pick(tm, MM_LN_ROW_CHUNK)
        for r in range(tm // rw):
            rows = slice(r * rw, (r + 1) * rw)
            xo_ref[rows, :] = _ln_rows(alpha * xres_ref[rows, :] + xo_ref[rows, :], g_ref[...], b_ref[...])


def _mm_res_ln(a, w, xres, g, b, alpha, tm, tk, name):
    M, K = a.shape
    D = w.shape[1]
    tm, tk = _pick(M, tm), _pick(K, tk)
    nk = K // tk
    row = pl.BlockSpec((tm, D), lambda i, k: (i, 0))
    vec = pl.BlockSpec((1, D), lambda i, k: (0, 0))
    return pl.pallas_call(
        functools.partial(_mm_res_ln_kernel, nk=nk, alpha=alpha), name=name,
        grid=(M // tm, nk),
        in_specs=[pl.BlockSpec((tm, tk), lambda i, k: (i, k)),
                  pl.BlockSpec((tk, D), lambda i, k: (k, 0)),
                  row, vec, vec],
        out_specs=row,
        out_shape=jax.ShapeDtypeStruct((M, D), F32),
        compiler_params=_params(2),
    )(a, w, xres, g.reshape(1, D), b.reshape(1, D))


def _gate_merge_kernel(gz_ref, oa_ref, ob_ref, oc_ref, od_ref, wup_ref, bg_ref, wbr_ref, o_ref):
    gz = gz_ref[...]
    acc = None
    for n, o_n in enumerate((oa_ref, ob_ref, oc_ref, od_ref)):
        z = jnp.dot(gz, wup_ref[n], preferred_element_type=F32) + bg_ref[n:n + 1, :]
        t = jnp.dot(o_n[...], wbr_ref[n], preferred_element_type=F32)
        term = t * (1.0 / (1.0 + jnp.exp(-z)))
        acc = term if acc is None else acc + term
    o_ref[...] = acc.astype(o_ref.dtype)


def _gate_merge(h, o_a, o_b, o_c, o_d, wup, bg, wbr, tm, tn):
    M = h.shape[0]
    D = wup.shape[2]
    tm, tn = _pick(M, tm), _pick(D, tn)
    o_spec = pl.BlockSpec((tm, BRANCH_W), lambda i, j: (i, 0))
    return pl.pallas_call(
        _gate_merge_kernel, name="gate_merge",
        grid=(M // tm, D // tn),
        in_specs=[pl.BlockSpec((tm, GATE_RANK), lambda i, j: (i, COL_GZ // GATE_RANK)),
                  o_spec, o_spec, o_spec, o_spec,
                  pl.BlockSpec((N_BRANCH, GATE_RANK, tn), lambda i, j: (0, 0, j)),
                  pl.BlockSpec((N_BRANCH, tn), lambda i, j: (0, j)),
                  pl.BlockSpec((N_BRANCH, BRANCH_W, tn), lambda i, j: (0, 0, j))],
        out_specs=pl.BlockSpec((tm, tn), lambda i, j: (i, j)),
        out_shape=jax.ShapeDtypeStruct((M, D), BF16),
        compiler_params=_params(2),
    )(h, o_a, o_b, o_c, o_d, wup, bg, wbr)


def _lane():
    return lax.broadcasted_iota(jnp.int32, (1, LANES), 1)


def _dot_nt(a, b):
    return lax.dot_general(a, b, (((1,), (1,)), ((), ())), preferred_element_type=F32)


def _split_heads(x):
    lo = (_lane() < HEAD_DIM).astype(F32)
    return jnp.concatenate([(x * lo).astype(BF16), (x * (1.0 - lo)).astype(BF16)], axis=0)


def _merge_heads(o, rows):
    return jnp.where(_lane() < HEAD_DIM, o[:rows], o[rows:])


def _group_mean(x):
    r = lax.broadcasted_iota(jnp.int32, (LANES, LANES), 0) >> 6
    c = lax.broadcasted_iota(jnp.int32, (LANES, LANES), 1) >> 6
    p = jnp.where(r == c, 1.0 / HEAD_DIM, 0.0).astype(BF16)
    hi = x.astype(BF16)
    lo = (x - hi.astype(F32)).astype(BF16)
    return jnp.dot(hi, p, preferred_element_type=F32) + jnp.dot(lo, p, preferred_element_type=F32)


def _rms_heads(x, g):
    return x * lax.rsqrt(_group_mean(x * x) + RMS_EPS) * g


FLASH_ROWS = 16
LOG2E = math.log2(math.e)


def _flash_scratch(rows, tk):
    return [pltpu.VMEM((rows, tk), F32), pltpu.VMEM((rows, tk), BF16)] + [pltpu.VMEM((rows, LANES), F32)] * 4


def _flash(qs, k_ref, v_ref, scratch, n_chunks, tk, bias_fn=None):
    s_ref, p_ref, m_ref, a_ref, l_ref, acc_ref = scratch
    rows = qs.shape[0]
    rb = FLASH_ROWS
    ncol = tk // LANES
    assert rows % rb == 0
    m_ref[...] = jnp.full(m_ref.shape, NEG_INF, F32)
    l_ref[...] = jnp.zeros(l_ref.shape, F32)
    acc_ref[...] = jnp.zeros(acc_ref.shape, F32)

    def chunk(t, carry):
        ks = pl.multiple_of(t * tk, tk)
        s = _dot_nt(qs, k_ref[pl.ds(ks, tk), :])
        mx = None
        for c in range(ncol):
            sc = s[:, c * LANES:(c + 1) * LANES]
            if bias_fn is not None:
                sc = sc + bias_fn(t, c)
            s_ref[:, c * LANES:(c + 1) * LANES] = sc
            mx = sc if mx is None else jnp.maximum(mx, sc)
        m_old = m_ref[...]
        m_new = jnp.maximum(m_old, jnp.broadcast_to(jnp.max(mx, axis=-1, keepdims=True), (rows, LANES)))
        m_ref[...] = m_new
        a_ref[...] = jnp.exp2(m_old - m_new)
        for i in range(rows // rb):
            rs = slice(i * rb, (i + 1) * rb)
            m_blk = m_ref[rs, :]
            p = [jnp.exp2(s_ref[rs, c * LANES:(c + 1) * LANES] - m_blk) for c in range(ncol)]
            l_ref[rs, :] = a_ref[rs, :] * l_ref[rs, :] + functools.reduce(jnp.add, p)
            for c in range(ncol):
                p_ref[rs, c * LANES:(c + 1) * LANES] = p[c].astype(BF16)
        acc_ref[...] = a_ref[...] * acc_ref[...] + jnp.dot(p_ref[...], v_ref[pl.ds(ks, tk), :],
                                                          preferred_element_type=F32)
        return carry

    lax.fori_loop(0, n_chunks, chunk, 0)
    return acc_ref[...] / jnp.sum(l_ref[...], axis=-1, keepdims=True)


def _t5_bucket(rel):
    half = T5_BUCKETS // 2
    max_exact = half // 2
    n = jnp.abs(rel)
    nf = jnp.maximum(n, 1).astype(jnp.float32)
    large = max_exact + (jnp.log(nf / max_exact) / math.log(T5_MAX_DIST / max_exact)
                         * (half - max_exact)).astype(jnp.int32)
    large = jnp.minimum(large, half - 1)
    return jnp.where(rel > 0, half, 0) + jnp.where(n < max_exact, n, large)


def _toeplitz(x, rows, cols):
    n = rows + cols
    pad = n - x.shape[-1]
    x = x[..., :n] if pad <= 0 else jnp.pad(x, [(0, 0)] * (x.ndim - 1) + [(0, pad)])
    t = jnp.tile(x, (1,) * (x.ndim - 1) + (rows,))[..., :rows * (n - 1)]
    return t.reshape(x.shape[:-1] + (rows, n - 1))[..., rows - 1:rows - 1 + cols]


def _na_bias_table(rpb):
    H = rpb.shape[0]
    qcol = jnp.arange(GRID_W, dtype=jnp.int32)
    kcol = jnp.arange(GRID_W, dtype=jnp.int32)
    c0 = jnp.clip(qcol - NA_COLS // 2, 0, GRID_W - NA_COLS)
    col_ok = (kcol[None, :] >= c0[:, None]) & (kcol[None, :] < c0[:, None] + NA_COLS)
    rel = jnp.arange(-(GRID_W - 1), GRID_W, dtype=jnp.int32)
    by_rel = rpb[:, :, jnp.clip(rel + NA_COLS - 1, 0, 2 * NA_COLS - 2)]
    rpb_col = jnp.where(col_ok, _toeplitz(by_rel, GRID_W, GRID_W), NEG_INF)
    t = jnp.stack([rpb_col[:, NA_ROWS - 1 - d:2 * NA_ROWS - 1 - d] for d in range(NA_ROWS)])
    return t.transpose(0, 1, 3, 2, 4).reshape(NA_ROWS, H, GRID_W, NA_ROWS * GRID_W).astype(F32)


def _na_kernel(q_ref, k_ref, v_ref, bias_ref, o_ref, *, R):
    r = pl.program_id(1)
    r0 = jnp.clip(r - NA_ROWS // 2, 0, R - NA_ROWS)
    delta = r - r0
    ks = pl.multiple_of(r0 * GRID_W, GRID_W)
    nk = NA_ROWS * GRID_W
    for c in range(BRANCH_W // LANES):
        cols = slice(c * LANES, (c + 1) * LANES)
        qs = _split_heads(q_ref[:, cols].astype(F32) * HEAD_DIM ** -0.5)
        s = _dot_nt(qs, k_ref[pl.ds(ks, nk), cols])
        s = s + jnp.concatenate([bias_ref[delta, 2 * c], bias_ref[delta, 2 * c + 1]], axis=0)
        m = jnp.max(s, axis=-1, keepdims=True)
        p = jnp.exp(s - m)
        l = jnp.sum(p, axis=-1, keepdims=True)
        o = jnp.dot(p.astype(BF16), v_ref[pl.ds(ks, nk), cols], preferred_element_type=F32) / l
        o_ref[:, cols] = _merge_heads(o, GRID_W).astype(o_ref.dtype)


def _na_attention(h, bias, B, T):
    R = T // GRID_W
    assert R >= NA_ROWS
    M = h.shape[0]
    return pl.pallas_call(
        functools.partial(_na_kernel, R=R), name="na_attn",
        grid=(B, R),
        in_specs=[pl.BlockSpec((GRID_W, BRANCH_W), lambda b, r: (b * R + r, COL_QA // BRANCH_W)),
                  pl.BlockSpec((T, BRANCH_W), lambda b, r: (b, COL_KA // BRANCH_W)),
                  pl.BlockSpec((T, BRANCH_W), lambda b, r: (b, COL_VA // BRANCH_W)),
                  pl.BlockSpec(bias.shape, lambda b, r: (0, 0, 0, 0))],
        out_specs=pl.BlockSpec((GRID_W, BRANCH_W), lambda b, r: (b * R + r, 0)),
        out_shape=jax.ShapeDtypeStruct((M, BRANCH_W), BF16),
        compiler_params=_params(2),
    )(h, h, h, bias)


DIFF_TQ, DIFF_TK = 256, 512
DIFF_D_LO = -(DIFF_TK + DIFF_TQ)
DIFF_D_HI = 2 * DIFF_TQ
assert DIFF_TQ >= T5_MAX_DIST and DIFF_TK % DIFF_TQ == 0


def _diff_bias_tiles(t5_tab):
    tq, tk = DIFF_TQ, DIFF_TK
    lo = DIFF_D_LO - (tq - 1)
    rel = jnp.arange(lo, DIFF_D_HI + tk, dtype=jnp.int32)
    by_rel = t5_tab[_t5_bucket(rel)].T
    offs = range(DIFF_D_LO, DIFF_D_HI + 1, tq)
    x = jnp.stack([by_rel[:, d - (tq - 1) - lo:d + tk - lo] for d in offs], axis=1)
    return (_toeplitz(x, tq, tk) * LOG2E).astype(F32)


def _diff_kernel(lam_ref, q_ref, k_ref, v_ref, tiles_ref, g_ref, o_ref, *scratch, n_chunks, out_scale):
    qi = pl.program_id(2)
    tq, tk = DIFF_TQ, DIFF_TK
    lane = _lane()
    qf = q_ref[...].astype(F32) * (DIFF_D ** -0.5 * LOG2E)
    qs = jnp.concatenate([(qf * ((lane >> 5) == v).astype(F32)).astype(BF16) for v in range(4)], axis=0)

    def bias_fn(kc, c):
        d = jnp.clip(kc * tk - qi * tq, DIFF_D_LO, DIFF_D_HI)
        j = (d - DIFF_D_LO) // tq
        b0 = tiles_ref[0, j, :, c * LANES:(c + 1) * LANES]
        b1 = tiles_ref[1, j, :, c * LANES:(c + 1) * LANES]
        return jnp.concatenate([b0, b0, b1, b1], axis=0)

    o = _flash(qs, k_ref, v_ref, scratch, n_chunks, tk, bias_fn)
    lam = lam_ref[0]
    x = jnp.where(lane < HEAD_DIM, o[:tq] - lam * o[tq:2 * tq], o[2 * tq:3 * tq] - lam * o[3 * tq:])
    o_ref[...] = (_rms_heads(x, g_ref[...]) * out_scale).astype(o_ref.dtype)


def _diff_attention(h, tiles, lam, g, lam_init, B, T):
    M = h.shape[0]
    tq, tk = DIFF_TQ, DIFF_TK
    assert T % tk == 0
    nq = T // tq
    qb, kb, vb = COL_QB // LANES, COL_KB // LANES, COL_VB // LANES
    g2 = jnp.tile(g.astype(F32), 2).reshape(1, LANES)
    return pl.pallas_call(
        functools.partial(_diff_kernel, n_chunks=T // tk, out_scale=1.0 - lam_init), name="diff_attn",
        grid=(B, DIFF_HEADS // 2, nq),
        in_specs=[pl.BlockSpec(memory_space=pltpu.SMEM),
                  pl.BlockSpec((tq, LANES), lambda b, hp, qi: (b * nq + qi, qb + hp)),
                  pl.BlockSpec((T, LANES), lambda b, hp, qi: (b, kb + hp)),
                  pl.BlockSpec((T, LANES), lambda b, hp, qi: (b, vb + hp)),
                  pl.BlockSpec((2,) + tiles.shape[1:], lambda b, hp, qi: (hp, 0, 0, 0)),
                  pl.BlockSpec((1, LANES), lambda b, hp, qi: (0, 0))],
        out_specs=pl.BlockSpec((tq, LANES), lambda b, hp, qi: (b * nq + qi, hp)),
        out_shape=jax.ShapeDtypeStruct((M, BRANCH_W), BF16),
        scratch_shapes=_flash_scratch(4 * tq, tk),
        compiler_params=_params(3),
    )(lam.reshape(1).astype(F32), h, h, h, tiles, g2)


GQA_TQ, GQA_TK = 128, 512


def _rope_tables(T):
    t = jnp.arange(T, dtype=jnp.int32)
    row = (t // GRID_W).astype(jnp.float32)
    col = (t % GRID_W).astype(jnp.float32)
    freqs = ROPE_THETA ** (-jnp.arange(0, ROPE_AXIS_DIM, 2, dtype=jnp.float32) / ROPE_AXIS_DIM)
    ang = jnp.concatenate([row[:, None] * freqs, col[:, None] * freqs], axis=-1)
    cos, sin = jnp.cos(ang), jnp.sin(ang)
    cos_t = jnp.tile(cos, (1, 4))
    sin_t = jnp.tile(jnp.concatenate([-sin, sin], axis=-1), (1, 2))
    return cos_t.astype(F32), sin_t.astype(F32)


def _rope(x, cos, sin):
    first = (_lane() & (HEAD_DIM - 1)) < HEAD_DIM // 2
    partner = jnp.where(first, pltpu.roll(x, LANES - HEAD_DIM // 2, axis=1), pltpu.roll(x, HEAD_DIM // 2, axis=1))
    return x * cos + partner * sin


def _gqa_kernel(q_ref, k_ref, v_ref, cq_ref, sq_ref, ck_ref, sk_ref, gq_ref, gk_ref, o_ref, kproc_ref, *scratch,
                n_chunks):
    qi = pl.program_id(1)
    tq = GQA_TQ

    @pl.when(qi == 0)
    def _():
        kf = _rms_heads(k_ref[...].astype(F32), gk_ref[...])
        kproc_ref[...] = _rope(kf, ck_ref[...], sk_ref[...]).astype(BF16)

    cq, sq, gq = cq_ref[...], sq_ref[...], gq_ref[...]
    parts = []
    for c in range(BRANCH_W // LANES):
        qf = _rms_heads(q_ref[:, c * LANES:(c + 1) * LANES].astype(F32), gq)
        parts.append(_split_heads(_rope(qf, cq, sq) * (HEAD_DIM ** -0.5 * LOG2E)))
    o = _flash(jnp.concatenate(parts, axis=0), kproc_ref, v_ref, scratch, n_chunks, GQA_TK)
    for c in range(BRANCH_W // LANES):
        o_ref[:, c * LANES:(c + 1) * LANES] = _merge_heads(o[2 * c * tq:(2 * c + 2) * tq], tq).astype(o_ref.dtype)


def _gqa_attention(h, cos_t, sin_t, gq, gk, B, T):
    M = h.shape[0]
    tq, tk = GQA_TQ, GQA_TK
    assert T % tk == 0
    nq = T // tq
    vec = pl.BlockSpec((1, LANES), lambda b, qi: (0, 0))
    qtab = pl.BlockSpec((tq, LANES), lambda b, qi: (qi, 0))
    ktab = pl.BlockSpec((T, LANES), lambda b, qi: (0, 0))
    return pl.pallas_call(
        functools.partial(_gqa_kernel, n_chunks=T // tk), name="gqa_attn",
        grid=(B, nq),
        in_specs=[pl.BlockSpec((tq, BRANCH_W), lambda b, qi: (b * nq + qi, COL_QC // BRANCH_W)),
                  pl.BlockSpec((T, LANES), lambda b, qi: (b, COL_KC // LANES)),
                  pl.BlockSpec((T, LANES), lambda b, qi: (b, COL_VC // LANES)),
                  qtab, qtab, ktab, ktab, vec, vec],
        out_specs=pl.BlockSpec((tq, BRANCH_W), lambda b, qi: (b * nq + qi, 0)),
        out_shape=jax.ShapeDtypeStruct((M, BRANCH_W), BF16),
        scratch_shapes=[pltpu.VMEM((T, LANES), BF16)] + _flash_scratch(GQA_HEADS * tq, tk),
        compiler_params=_params(2),
    )(h, h, h, cos_t, sin_t, cos_t, sin_t, gq, gk)


DIL_TQ = 128
DIL_HALF = 64
assert all(w // (2 * d) == DIL_HALF for w, d in DIL_GROUPS)


def _dil_bias_tiles(t5_tab, g, L):
    dil = DIL_GROUPS[g][1]
    tq = DIL_TQ
    W = min(2 * tq, L)
    offs = range(0, -(W - tq) - 1, -DIL_HALF)
    lo = offs[-1] - (tq - 1)
    rel = jnp.arange(lo, W, dtype=jnp.int32)
    s0 = DIFF_HEADS + g * DIL_SLOTS
    by_rel = t5_tab[:, s0:s0 + DIL_SLOTS][_t5_bucket(rel * dil)]
    by_rel = jnp.where((jnp.abs(rel) <= DIL_HALF)[:, None], by_rel, NEG_INF).T
    x = jnp.stack([by_rel[:, d - (tq - 1) - lo:d + W - lo] for d in offs])
    return _toeplitz(x, tq, W).astype(F32)


def _dil_kernel(q_ref, k_ref, v_ref, tiles_ref, o_ref, lse_ref, *, L, W):
    n = pl.program_id(2)
    tq = DIL_TQ
    ws = jnp.clip(n * tq - DIL_HALF, 0, L - W)
    var = (n * tq - ws) // DIL_HALF
    ws = pl.multiple_of(ws, DIL_HALF)
    for c in range(BRANCH_W // LANES):
        cols = slice(c * LANES, (c + 1) * LANES)
        qs = _split_heads(q_ref[:, cols].astype(F32) * HEAD_DIM ** -0.5)
        s = _dot_nt(qs, k_ref[pl.ds(ws, W), cols])
        s = s + jnp.concatenate([tiles_ref[var, 2 * c], tiles_ref[var, 2 * c + 1]], axis=0)
        m = jnp.max(s, axis=-1, keepdims=True)
        p = jnp.exp(s - m)
        l = jnp.sum(p, axis=-1, keepdims=True)
        o = jnp.dot(p.astype(BF16), v_ref[pl.ds(ws, W), cols], preferred_element_type=F32) / l
        lse = jnp.broadcast_to(m + jnp.log(l), (2 * tq, LANES))
        o_ref[:, cols] = _merge_heads(o, tq).astype(o_ref.dtype)
        lse_ref[:, cols] = _merge_heads(lse, tq)


def _dil_attention(hd, col0, tiles, B, T):
    dil, Md, _ = hd.shape
    L = T // dil
    tq = DIL_TQ
    W = min(2 * tq, L)
    assert L % tq == 0
    nq = L // tq
    out_spec = pl.BlockSpec((None, tq, BRANCH_W), lambda b, rho, n: (rho, b * nq + n, 0))
    return pl.pallas_call(
        functools.partial(_dil_kernel, L=L, W=W), name=f"dil_attn_{dil}",
        grid=(B, dil, nq),
        in_specs=[pl.BlockSpec((None, tq, BRANCH_W), lambda b, rho, n: (rho, b * nq + n, col0)),
                  pl.BlockSpec((None, L, BRANCH_W), lambda b, rho, n: (rho, b, col0 + 1)),
                  pl.BlockSpec((None, L, BRANCH_W), lambda b, rho, n: (rho, b, col0 + 2)),
                  pl.BlockSpec(tiles.shape, lambda b, rho, n: (0, 0, 0, 0))],
        out_specs=[out_spec, out_spec],
        out_shape=[jax.ShapeDtypeStruct((dil, Md, BRANCH_W), BF16), jax.ShapeDtypeStruct((dil, Md, BRANCH_W), F32)],
        compiler_params=_params(3),
    )(hd, hd, hd, tiles)


def _dil_merge_kernel(o0_ref, o1_ref, o2_ref, l0_ref, l1_ref, l2_ref, o_ref, *scratch):
    def natural(ref, scr):
        dil, rows, _ = ref.shape
        for c in range(scr.shape[0]):
            for rho in range(dil):
                scr[c, pl.ds(rho, rows, stride=dil), :] = ref[rho, :, c * LANES:(c + 1) * LANES].astype(F32)
        return jnp.concatenate([scr[c] for c in range(scr.shape[0])], axis=1)

    os_ = [o0_ref[0].astype(F32), natural(o1_ref, scratch[0]), natural(o2_ref, scratch[1])]
    ls = [l0_ref[0], natural(l1_ref, scratch[2]), natural(l2_ref, scratch[3])]
    mx = jnp.maximum(jnp.maximum(ls[0], ls[1]), ls[2])
    es = [jnp.exp(l - mx) for l in ls]
    den = es[0] + es[1] + es[2]
    num = es[0] * os_[0] + es[1] * os_[1] + es[2] * os_[2]
    o_ref[...] = (num / den).astype(o_ref.dtype)


def _dil_merge(outs, lses):
    M = outs[0].shape[1]
    tm = _pick(M, 512)

    def spec(a):
        dil = a.shape[0]
        return pl.BlockSpec((dil, tm // dil, BRANCH_W), lambda i: (0, i, 0))

    return pl.pallas_call(
        _dil_merge_kernel, name="dil_merge",
        grid=(M // tm,),
        in_specs=[spec(a) for a in (*outs, *lses)],
        out_specs=pl.BlockSpec((tm, BRANCH_W), lambda i: (i, 0)),
        out_shape=jax.ShapeDtypeStruct((M, BRANCH_W), BF16),
        scratch_shapes=[pltpu.VMEM((BRANCH_W // LANES, tm, LANES), F32)] * 4,
        compiler_params=_params(1),
    )(*outs, *lses)


def _deinterleave(w, n_heads):
    d = w.shape[0]
    return w.reshape(d, n_heads, HEAD_DIM // 2, 2).transpose(0, 1, 3, 2).reshape(d, n_heads * HEAD_DIM)


def _prep_layer(l, w_in, w_gate_down, w_gate_up, w_branch, w_out, w_ffn_in, w_ffn_out, ff_pad):
    d = w_in.shape[1]
    wi = w_in[l]
    qc = _deinterleave(wi[:, 3072:3584], GQA_HEADS).reshape(d, GQA_HEADS, HEAD_DIM)
    qc = qc[:, jnp.array(GQA_HEAD_ORDER)].reshape(d, GQA_HEADS * HEAD_DIM)
    kc = _deinterleave(wi[:, 3584:3712], GQA_KV_HEADS)
    qd, kd, vd = wi[:, 3840:5376], wi[:, 5376:6912], wi[:, 6912:8448]

    def group(g):
        s = slice(g * BRANCH_W, (g + 1) * BRANCH_W)
        return [qd[:, s], kd[:, s], vd[:, s]]

    w_proj = jnp.concatenate([wi[:, :3072], qc, kc, wi[:, 3712:3840], w_gate_down[l]]
                             + group(0) + group(1) + group(2), axis=1)
    assert w_proj.shape[1] == H_MAIN_COLS + 6 * BRANCH_W
    wup = w_gate_up[l].reshape(GATE_RANK, N_BRANCH, d).transpose(1, 0, 2)
    wbr = w_branch[l]
    wbr_c = wbr[2].reshape(GQA_HEADS, HEAD_DIM, d)[jnp.array(GQA_HEAD_ORDER)].reshape(BRANCH_W, d)
    wbr = jnp.stack([wbr[0], wbr[1], wbr_c, wbr[3]])
    ff = w_ffn_out.shape[1]
    wfi = w_ffn_in[l]
    pad = ff_pad - ff
    wfi = jnp.concatenate([jnp.pad(wfi[:, :ff], ((0, 0), (0, pad))), jnp.pad(wfi[:, ff:], ((0, 0), (0, pad)))], axis=1)
    wfo = jnp.pad(w_ffn_out[l], ((0, pad), (0, 0)))
    return dict(w_proj=w_proj.astype(BF16), wup=wup.astype(BF16), wbr=wbr.astype(BF16),
                w_out=w_out[l].astype(BF16), wfi=wfi.astype(BF16), wfo=wfo.astype(BF16))


def _trunk(x, consts, layers, p):
    B, T, D = x.shape
    M = B * T
    depth = len(layers)
    alpha = (2 * depth) ** 0.25
    cos_t, sin_t = _rope_tables(T)
    dil_tiles = [_dil_bias_tiles(p['t5_table'], g, T // dil) for g, (_, dil) in enumerate(DIL_GROUPS)]
    xf = _ln_in(x.reshape(M, D), p['ln_in_g'], p['ln_in_b'])
    for l, lw in enumerate(layers):
        lam_init = 0.8 - 0.6 * math.exp(-0.3 * l)
        h3, hd1, hd2 = _projections(xf, lw['w_proj'], 512, 512)
        h = h3[0]
        o_a = _na_attention(h, consts['na_bias'][l], B, T)
        lq = p['diff_lambda'][l].astype(F32)
        lam = jnp.exp(jnp.sum(lq[0] * lq[1])) - jnp.exp(jnp.sum(lq[2] * lq[3])) + lam_init
        o_b = _diff_attention(h, consts['diff_tiles'], lam, p['diff_subln_g'][l], lam_init, B, T)
        gq = jnp.tile(jnp.concatenate([p['qk_norm_g'][l, 0, 0::2], p['qk_norm_g'][l, 0, 1::2]]), 2).reshape(1, LANES)
        gk = jnp.tile(jnp.concatenate([p['qk_norm_g'][l, 1, 0::2], p['qk_norm_g'][l, 1, 1::2]]), 2).reshape(1, LANES)
        o_c = _gqa_attention(h, cos_t, sin_t, gq.astype(F32), gk.astype(F32), B, T)
        dil_out = [_dil_attention(h3, COL_D0 // BRANCH_W, dil_tiles[0], B, T),
                   _dil_attention(hd1, 0, dil_tiles[1], B, T),
                   _dil_attention(hd2, 0, dil_tiles[2], B, T)]
        o_d = _dil_merge([o for o, _ in dil_out], [s for _, s in dil_out])
        merged = _gate_merge(h, o_a, o_b, o_c, o_d, lw['wup'], p['b_gate'][l].astype(F32), lw['wbr'], 512, 1024)
        xf = _mm_res_ln(merged, lw['w_out'], xf, p['ln1_g'][l], p['ln1_b'][l], alpha, 512, 512, "out_proj_ln")
        ff = _swiglu(xf, lw['wfi'], 512, 512)
        xf = _mm_res_ln(ff, lw['wfo'], xf, p['ln2_g'][l], p['ln2_b'][l], alpha, 512, 512, "ffn_out_ln")
    return xf.reshape(B, T, D)


def kernel(x_prompt, x_sample, ln_in_g, ln_in_b, w_in, na_rpb, qk_norm_g, diff_lambda, diff_subln_g, t5_table,
           w_gate_down, w_gate_up, b_gate, w_branch, w_out, ln1_g, ln1_b, w_ffn_in, w_ffn_out, ln2_g, ln2_b):
    depth = w_in.shape[0]
    ff = w_ffn_out.shape[1]
    ff_pad = -(-ff // 1024) * 1024
    layers = [_prep_layer(l, w_in, w_gate_down, w_gate_up, w_branch, w_out, w_ffn_in, w_ffn_out, ff_pad)
              for l in range(depth)]
    consts = dict(na_bias=[_na_bias_table(na_rpb[l]) for l in range(depth)],
                  diff_tiles=_diff_bias_tiles(t5_table[:, :DIFF_HEADS]))
    p = dict(ln_in_g=ln_in_g, ln_in_b=ln_in_b, qk_norm_g=qk_norm_g, diff_lambda=diff_lambda,
             diff_subln_g=diff_subln_g, t5_table=t5_table, b_gate=b_gate, ln1_g=ln1_g, ln1_b=ln1_b,
             ln2_g=ln2_g, ln2_b=ln2_b)
    return (_trunk(x_prompt, consts, layers, p), _trunk(x_sample, consts, layers, p))
```

```python
import functools
import math

import jax
import jax.numpy as jnp
from jax import lax
from jax.experimental import pallas as pl
from jax.experimental.pallas import tpu as pltpu

F32 = jnp.float32
BF16 = jnp.bfloat16

HEAD_DIM = 64
LANES = 128
GRID_W = 64
NA_ROWS = 8
NA_COLS = 16
DIFF_HEADS = 8
DIFF_D = HEAD_DIM // 2
GQA_HEADS = 8
GQA_KV_HEADS = 2
ROPE_AXIS_DIM = HEAD_DIM // 2
ROPE_THETA = 10000.0
DIL_SLOTS = 8
DIL_GROUPS = ((128, 1), (512, 4), (2048, 16))
T5_BUCKETS = 32
T5_MAX_DIST = 128
N_BRANCH = 4
BRANCH_W = 8 * HEAD_DIM
GATE_RANK = 256
LN_EPS = 1e-5
RMS_EPS = 1e-6
NEG_INF = -1e30

COL_QA, COL_KA, COL_VA = 0, 512, 1024
COL_QB, COL_KB, COL_VB = 1536, 2048, 2560
COL_QC = 3072
COL_KC, COL_VC = 3584, 3712
COL_GZ = 3840
COL_D0 = 4096
H_MAIN_COLS = COL_D0 + 3 * BRANCH_W
GQA_HEAD_ORDER = (0, 4, 1, 5, 2, 6, 3, 7)

VMEM_LIMIT = 56 * 1024 * 1024


def _params(n_axes, vmem=VMEM_LIMIT):
    return pltpu.CompilerParams(dimension_semantics=("arbitrary",) * n_axes, vmem_limit_bytes=vmem)


def _pick(n, pref):
    t = min(n, pref)
    while n % t:
        t //= 2
    return t


def _ln_rows(y, g, b):
    mu = jnp.mean(y, axis=-1, keepdims=True)
    yc = y - mu
    var = jnp.mean(yc * yc, axis=-1, keepdims=True)
    return yc * lax.rsqrt(var + LN_EPS) * g + b


def _ln_in_kernel(x_ref, g_ref, b_ref, xo_ref, xb_ref):
    r = _ln_rows(x_ref[...], g_ref[...], b_ref[...])
    xo_ref[...] = r
    xb_ref[...] = r.astype(BF16)


def _ln_in(x, g, b):
    M, D = x.shape
    tm = _pick(M, 256)
    row = pl.BlockSpec((tm, D), lambda i: (i, 0))
    vec = pl.BlockSpec((1, D), lambda i: (0, 0))
    return pl.pallas_call(
        _ln_in_kernel, name="ln_in",
        grid=(M // tm,),
        in_specs=[row, vec, vec],
        out_specs=[row, row],
        out_shape=[jax.ShapeDtypeStruct((M, D), F32), jax.ShapeDtypeStruct((M, D), BF16)],
        compiler_params=_params(1),
    )(x, g.reshape(1, D), b.reshape(1, D))


def _proj_kernel(x_ref, w_ref, om_ref, o1_ref, o2_ref, r_ref, *, n_main, n_dil):
    j = pl.program_id(1)
    r = jnp.dot(x_ref[...], w_ref[...], preferred_element_type=F32)

    def residue_major(o_ref):
        dil, rows, _ = o_ref.shape
        for c in range(r_ref.shape[0]):
            cols = slice(c * LANES, (c + 1) * LANES)
            r_ref[c] = r[:, cols]
            for rho in range(dil):
                o_ref[rho, :, cols] = r_ref[c, pl.ds(rho, rows, stride=dil), :].astype(o_ref.dtype)

    @pl.when(j < n_main)
    def _():
        om_ref[0] = r.astype(om_ref.dtype)

    @pl.when((j >= n_main) & (j < n_main + n_dil))
    def _():
        residue_major(o1_ref)

    @pl.when(j >= n_main + n_dil)
    def _():
        residue_major(o2_ref)


def _projections(x, w, tm, tn):
    M, K = x.shape
    tm, tn = _pick(M, tm), _pick(3 * BRANCH_W, tn)
    n_main, n_dil = H_MAIN_COLS // tn, 3 * BRANCH_W // tn
    d1, d2 = DIL_GROUPS[1][1], DIL_GROUPS[2][1]
    assert w.shape[1] == (n_main + 2 * n_dil) * tn and tm % d2 == 0

    def out_spec(dil, first, count):
        return pl.BlockSpec((dil, tm // dil, tn), lambda i, j: (0, i, jnp.clip(j - first, 0, count - 1)))

    return pl.pallas_call(
        functools.partial(_proj_kernel, n_main=n_main, n_dil=n_dil), name="projections",
        grid=(M // tm, n_main + 2 * n_dil),
        in_specs=[pl.BlockSpec((tm, K), lambda i, j: (i, 0)), pl.BlockSpec((K, tn), lambda i, j: (0, j))],
        out_specs=[out_spec(1, 0, n_main), out_spec(d1, n_main, n_dil), out_spec(d2, n_main + n_dil, n_dil)],
        out_shape=[jax.ShapeDtypeStruct((1, M, H_MAIN_COLS), BF16),
                   jax.ShapeDtypeStruct((d1, M // d1, 3 * BRANCH_W), BF16),
                   jax.ShapeDtypeStruct((d2, M // d2, 3 * BRANCH_W), BF16)],
        scratch_shapes=[pltpu.VMEM((tn // LANES, tm, LANES), F32)],
        compiler_params=_params(2),
    )(x, w)


def _swiglu_kernel(x_ref, wa_ref, wb_ref, o_ref):
    x = x_ref[...]
    a = jnp.dot(x, wa_ref[...], preferred_element_type=F32)
    b = jnp.dot(x, wb_ref[...], preferred_element_type=F32)
    o_ref[...] = (a * (1.0 / (1.0 + jnp.exp(-a))) * b).astype(o_ref.dtype)


def _swiglu(x, w, tm, tn):
    M, K = x.shape
    F = w.shape[1] // 2
    tm, tn = _pick(M, tm), _pick(F, tn)
    nf = F // tn
    return pl.pallas_call(
        _swiglu_kernel, name="swiglu",
        grid=(M // tm, nf),
        in_specs=[pl.BlockSpec((tm, K), lambda i, j: (i, 0)),
                  pl.BlockSpec((K, tn), lambda i, j: (0, j)),
                  pl.BlockSpec((K, tn), lambda i, j: (0, j + nf))],
        out_specs=pl.BlockSpec((tm, tn), lambda i, j: (i, j)),
        out_shape=jax.ShapeDtypeStruct((M, F), BF16),
        compiler_params=_params(2),
    )(x, w, w)


MM_LN_COL_CHUNK = 1024
MM_LN_ROW_CHUNK = 128
MM_LN_UNROLL = 1


def _mm_res_ln_kernel(a_ref, w_ref, xres_ref, g_ref, b_ref, xo_ref, xb_ref, *, nk, alpha):
    k = pl.program_id(1)
    tm, D = xo_ref.shape

    cw = _pick(D, MM_LN_COL_CHUNK)

    def partial_products(accumulate):
        a = a_ref[...]
        for c in range(D // cw):
            cols = slice(c * cw, (c + 1) * cw)
            part = jnp.dot(a, w_ref[:, cols], preferred_element_type=F32)
            xo_ref[:, cols] = xo_ref[:, cols] + part if accumulate else part

    @pl.when(k == 0)
    def _():
        partial_products(False)

    @pl.when(k > 0)
    def _():
        partial_products(True)

    @pl.when(k == nk - 1)
    def _():
        rw = _pick(tm, MM_LN_ROW_CHUNK)

        def ln_chunk(r, carry):
            rows = pl.ds(pl.multiple_of(r * rw, rw), rw)
            res = _ln_rows(alpha * xres_ref[rows, :] + xo_ref[rows, :], g_ref[...], b_ref[...])
            xo_ref[rows, :] = res
            xb_ref[rows, :] = res.astype(BF16)
            return carry

        lax.fori_loop(0, tm // rw, ln_chunk, 0, unroll=MM_LN_UNROLL)


def _mm_res_ln(a, w, xres, g, b, alpha, tm, tk, name):
    M, K = a.shape
    D = w.shape[1]
    tm, tk = _pick(M, tm), _pick(K, tk)
    nk = K // tk
    row = pl.BlockSpec((tm, D), lambda i, k: (i, 0))
    vec = pl.BlockSpec((1, D), lambda i, k: (0, 0))
    return pl.pallas_call(
        functools.partial(_mm_res_ln_kernel, nk=nk, alpha=alpha), name=name,
        grid=(M // tm, nk),
        in_specs=[pl.BlockSpec((tm, tk), lambda i, k: (i, k)),
                  pl.BlockSpec((tk, D), lambda i, k: (k, 0)),
                  row, vec, vec],
        out_specs=[row, row],
        out_shape=[jax.ShapeDtypeStruct((M, D), F32), jax.ShapeDtypeStruct((M, D), BF16)],
        compiler_params=_params(2),
    )(a, w, xres, g.reshape(1, D), b.reshape(1, D))


def _gate_merge_kernel(gz_ref, oa_ref, ob_ref, oc_ref, od_ref, wup_ref, bg_ref, wbr_ref, o_ref):
    gz = gz_ref[...]
    acc = None
    for n, o_n in enumerate((oa_ref, ob_ref, oc_ref, od_ref)):
        z = jnp.dot(gz, wup_ref[n], preferred_element_type=F32) + bg_ref[n:n + 1, :]
        t = jnp.dot(o_n[...], wbr_ref[n], preferred_element_type=F32)
        term = t * (1.0 / (1.0 + jnp.exp(-z)))
        acc = term if acc is None else acc + term
    o_ref[...] = acc.astype(o_ref.dtype)


def _gate_merge(h, o_a, o_b, o_c, o_d, wup, bg, wbr, tm, tn):
    M = h.shape[0]
    D = wup.shape[2]
    tm, tn = _pick(M, tm), _pick(D, tn)
    o_spec = pl.BlockSpec((tm, BRANCH_W), lambda i, j: (i, 0))
    return pl.pallas_call(
        _gate_merge_kernel, name="gate_merge",
        grid=(M // tm, D // tn),
        in_specs=[pl.BlockSpec((tm, GATE_RANK), lambda i, j: (i, COL_GZ // GATE_RANK)),
                  o_spec, o_spec, o_spec, o_spec,
                  pl.BlockSpec((N_BRANCH, GATE_RANK, tn), lambda i, j: (0, 0, j)),
                  pl.BlockSpec((N_BRANCH, tn), lambda i, j: (0, j)),
                  pl.BlockSpec((N_BRANCH, BRANCH_W, tn), lambda i, j: (0, 0, j))],
        out_specs=pl.BlockSpec((tm, tn), lambda i, j: (i, j)),
        out_shape=jax.ShapeDtypeStruct((M, D), BF16),
        compiler_params=_params(2),
    )(h, o_a, o_b, o_c, o_d, wup, bg, wbr)


def _lane():
    return lax.broadcasted_iota(jnp.int32, (1, LANES), 1)


def _dot_nt(a, b):
    return lax.dot_general(a, b, (((1,), (1,)), ((), ())), preferred_element_type=F32)


def _split_heads(x):
    lo = (_lane() < HEAD_DIM).astype(F32)
    return jnp.concatenate([(x * lo).astype(BF16), (x * (1.0 - lo)).astype(BF16)], axis=0)


def _merge_heads(o, rows):
    return jnp.where(_lane() < HEAD_DIM, o[:rows], o[rows:])


def _group_mean(x):
    r = lax.broadcasted_iota(jnp.int32, (LANES, LANES), 0) >> 6
    c = lax.broadcasted_iota(jnp.int32, (LANES, LANES), 1) >> 6
    p = jnp.where(r == c, 1.0 / HEAD_DIM, 0.0).astype(BF16)
    hi = x.astype(BF16)
    lo = (x - hi.astype(F32)).astype(BF16)
    return jnp.dot(hi, p, preferred_element_type=F32) + jnp.dot(lo, p, preferred_element_type=F32)


def _rms_heads(x, g):
    return x * lax.rsqrt(_group_mean(x * x) + RMS_EPS) * g


FLASH_ROWS = 16
LOG2E = math.log2(math.e)


def _flash_scratch(rows, tk):
    return [pltpu.VMEM((rows, tk), F32), pltpu.VMEM((rows, tk), BF16)] + [pltpu.VMEM((rows, LANES), F32)] * 4


def _flash(qs, k_ref, v_ref, scratch, n_chunks, tk, bias_fn=None):
    s_ref, p_ref, m_ref, a_ref, l_ref, acc_ref = scratch
    rows = qs.shape[0]
    rb = FLASH_ROWS
    ncol = tk // LANES
    assert rows % rb == 0
    m_ref[...] = jnp.full(m_ref.shape, NEG_INF, F32)
    l_ref[...] = jnp.zeros(l_ref.shape, F32)
    acc_ref[...] = jnp.zeros(acc_ref.shape, F32)

    def chunk(t, carry):
        ks = pl.multiple_of(t * tk, tk)
        s = _dot_nt(qs, k_ref[pl.ds(ks, tk), :])
        mx = None
        for c in range(ncol):
            sc = s[:, c * LANES:(c + 1) * LANES]
            if bias_fn is not None:
                sc = sc + bias_fn(t, c)
            s_ref[:, c * LANES:(c + 1) * LANES] = sc
            mx = sc if mx is None else jnp.maximum(mx, sc)
        m_old = m_ref[...]
        m_new = jnp.maximum(m_old, jnp.broadcast_to(jnp.max(mx, axis=-1, keepdims=True), (rows, LANES)))
        m_ref[...] = m_new
        a_ref[...] = jnp.exp2(m_old - m_new)
        for i in range(rows // rb):
            rs = slice(i * rb, (i + 1) * rb)
            m_blk = m_ref[rs, :]
            p = [jnp.exp2(s_ref[rs, c * LANES:(c + 1) * LANES] - m_blk) for c in range(ncol)]
            l_ref[rs, :] = a_ref[rs, :] * l_ref[rs, :] + functools.reduce(jnp.add, p)
            for c in range(ncol):
                p_ref[rs, c * LANES:(c + 1) * LANES] = p[c].astype(BF16)
        acc_ref[...] = a_ref[...] * acc_ref[...] + jnp.dot(p_ref[...], v_ref[pl.ds(ks, tk), :],
                                                          preferred_element_type=F32)
        return carry

    lax.fori_loop(0, n_chunks, chunk, 0)
    return acc_ref[...] / jnp.sum(l_ref[...], axis=-1, keepdims=True)


def _t5_bucket(rel):
    half = T5_BUCKETS // 2
    max_exact = half // 2
    n = jnp.abs(rel)
    nf = jnp.maximum(n, 1).astype(jnp.float32)
    large = max_exact + (jnp.log(nf / max_exact) / math.log(T5_MAX_DIST / max_exact)
                         * (half - max_exact)).astype(jnp.int32)
    large = jnp.minimum(large, half - 1)
    return jnp.where(rel > 0, half, 0) + jnp.where(n < max_exact, n, large)


def _toeplitz(x, rows, cols):
    n = rows + cols
    pad = n - x.shape[-1]
    x = x[..., :n] if pad <= 0 else jnp.pad(x, [(0, 0)] * (x.ndim - 1) + [(0, pad)])
    t = jnp.tile(x, (1,) * (x.ndim - 1) + (rows,))[..., :rows * (n - 1)]
    return t.reshape(x.shape[:-1] + (rows, n - 1))[..., rows - 1:rows - 1 + cols]


def _na_bias_table(rpb):
    H = rpb.shape[0]
    qcol = jnp.arange(GRID_W, dtype=jnp.int32)
    kcol = jnp.arange(GRID_W, dtype=jnp.int32)
    c0 = jnp.clip(qcol - NA_COLS // 2, 0, GRID_W - NA_COLS)
    col_ok = (kcol[None, :] >= c0[:, None]) & (kcol[None, :] < c0[:, None] + NA_COLS)
    rel = jnp.arange(-(GRID_W - 1), GRID_W, dtype=jnp.int32)
    by_rel = rpb[:, :, jnp.clip(rel + NA_COLS - 1, 0, 2 * NA_COLS - 2)]
    rpb_col = jnp.where(col_ok, _toeplitz(by_rel, GRID_W, GRID_W), NEG_INF)
    t = jnp.stack([rpb_col[:, NA_ROWS - 1 - d:2 * NA_ROWS - 1 - d] for d in range(NA_ROWS)])
    return t.transpose(0, 1, 3, 2, 4).reshape(NA_ROWS, H, GRID_W, NA_ROWS * GRID_W).astype(F32)


def _na_kernel(q_ref, k_ref, v_ref, bias_ref, o_ref, *, R):
    r = pl.program_id(1)
    r0 = jnp.clip(r - NA_ROWS // 2, 0, R - NA_ROWS)
    delta = r - r0
    ks = pl.multiple_of(r0 * GRID_W, GRID_W)
    nk = NA_ROWS * GRID_W
    for c in range(BRANCH_W // LANES):
        cols = slice(c * LANES, (c + 1) * LANES)
        qs = _split_heads(q_ref[:, cols].astype(F32) * HEAD_DIM ** -0.5)
        s = _dot_nt(qs, k_ref[pl.ds(ks, nk), cols])
        s = s + jnp.concatenate([bias_ref[delta, 2 * c], bias_ref[delta, 2 * c + 1]], axis=0)
        m = jnp.max(s, axis=-1, keepdims=True)
        p = jnp.exp(s - m)
        l = jnp.sum(p, axis=-1, keepdims=True)
        o = jnp.dot(p.astype(BF16), v_ref[pl.ds(ks, nk), cols], preferred_element_type=F32) / l
        o_ref[:, cols] = _merge_heads(o, GRID_W).astype(o_ref.dtype)


def _na_attention(h, bias, B, T):
    R = T // GRID_W
    assert R >= NA_ROWS
    M = h.shape[0]
    return pl.pallas_call(
        functools.partial(_na_kernel, R=R), name="na_attn",
        grid=(B, R),
        in_specs=[pl.BlockSpec((GRID_W, BRANCH_W), lambda b, r: (b * R + r, COL_QA // BRANCH_W)),
                  pl.BlockSpec((T, BRANCH_W), lambda b, r: (b, COL_KA // BRANCH_W)),
                  pl.BlockSpec((T, BRANCH_W), lambda b, r: (b, COL_VA // BRANCH_W)),
                  pl.BlockSpec(bias.shape, lambda b, r: (0, 0, 0, 0))],
        out_specs=pl.BlockSpec((GRID_W, BRANCH_W), lambda b, r: (b * R + r, 0)),
        out_shape=jax.ShapeDtypeStruct((M, BRANCH_W), BF16),
        compiler_params=_params(2),
    )(h, h, h, bias)


DIFF_TQ, DIFF_TK = 256, 512
DIFF_D_LO = -(DIFF_TK + DIFF_TQ)
DIFF_D_HI = 2 * DIFF_TQ
assert DIFF_TQ >= T5_MAX_DIST and DIFF_TK % DIFF_TQ == 0


def _diff_bias_tiles(t5_tab):
    tq, tk = DIFF_TQ, DIFF_TK
    lo = DIFF_D_LO - (tq - 1)
    rel = jnp.arange(lo, DIFF_D_HI + tk, dtype=jnp.int32)
    by_rel = t5_tab[_t5_bucket(rel)].T
    offs = range(DIFF_D_LO, DIFF_D_HI + 1, tq)
    x = jnp.stack([by_rel[:, d - (tq - 1) - lo:d + tk - lo] for d in offs], axis=1)
    return (_toeplitz(x, tq, tk) * LOG2E).astype(F32)


def _diff_kernel(lam_ref, q_ref, k_ref, v_ref, tiles_ref, g_ref, o_ref, *scratch, n_chunks, out_scale):
    qi = pl.program_id(2)
    tq, tk = DIFF_TQ, DIFF_TK
    lane = _lane()
    qf = q_ref[...].astype(F32) * (DIFF_D ** -0.5 * LOG2E)
    qs = jnp.concatenate([(qf * ((lane >> 5) == v).astype(F32)).astype(BF16) for v in range(4)], axis=0)

    def bias_fn(kc, c):
        d = jnp.clip(kc * tk - qi * tq, DIFF_D_LO, DIFF_D_HI)
        j = (d - DIFF_D_LO) // tq
        b0 = tiles_ref[0, j, :, c * LANES:(c + 1) * LANES]
        b1 = tiles_ref[1, j, :, c * LANES:(c + 1) * LANES]
        return jnp.concatenate([b0, b0, b1, b1], axis=0)

    o = _flash(qs, k_ref, v_ref, scratch, n_chunks, tk, bias_fn)
    lam = lam_ref[0]
    x = jnp.where(lane < HEAD_DIM, o[:tq] - lam * o[tq:2 * tq], o[2 * tq:3 * tq] - lam * o[3 * tq:])
    o_ref[...] = (_rms_heads(x, g_ref[...]) * out_scale).astype(o_ref.dtype)


def _diff_attention(h, tiles, lam, g, lam_init, B, T):
    M = h.shape[0]
    tq, tk = DIFF_TQ, DIFF_TK
    assert T % tk == 0
    nq = T // tq
    qb, kb, vb = COL_QB // LANES, COL_KB // LANES, COL_VB // LANES
    g2 = jnp.tile(g.astype(F32), 2).reshape(1, LANES)
    return pl.pallas_call(
        functools.partial(_diff_kernel, n_chunks=T // tk, out_scale=1.0 - lam_init), name="diff_attn",
        grid=(B, DIFF_HEADS // 2, nq),
        in_specs=[pl.BlockSpec(memory_space=pltpu.SMEM),
                  pl.BlockSpec((tq, LANES), lambda b, hp, qi: (b * nq + qi, qb + hp)),
                  pl.BlockSpec((T, LANES), lambda b, hp, qi: (b, kb + hp)),
                  pl.BlockSpec((T, LANES), lambda b, hp, qi: (b, vb + hp)),
                  pl.BlockSpec((2,) + tiles.shape[1:], lambda b, hp, qi: (hp, 0, 0, 0)),
                  pl.BlockSpec((1, LANES), lambda b, hp, qi: (0, 0))],
        out_specs=pl.BlockSpec((tq, LANES), lambda b, hp, qi: (b * nq + qi, hp)),
        out_shape=jax.ShapeDtypeStruct((M, BRANCH_W), BF16),
        scratch_shapes=_flash_scratch(4 * tq, tk),
        compiler_params=_params(3),
    )(lam.reshape(1).astype(F32), h, h, h, tiles, g2)


GQA_TQ, GQA_TK = 128, 512


def _rope_tables(T):
    t = jnp.arange(T, dtype=jnp.int32)
    row = (t // GRID_W).astype(jnp.float32)
    col = (t % GRID_W).astype(jnp.float32)
    freqs = ROPE_THETA ** (-jnp.arange(0, ROPE_AXIS_DIM, 2, dtype=jnp.float32) / ROPE_AXIS_DIM)
    ang = jnp.concatenate([row[:, None] * freqs, col[:, None] * freqs], axis=-1)
    cos, sin = jnp.cos(ang), jnp.sin(ang)
    cos_t = jnp.tile(cos, (1, 4))
    sin_t = jnp.tile(jnp.concatenate([-sin, sin], axis=-1), (1, 2))
    return cos_t.astype(F32), sin_t.astype(F32)


def _rope(x, cos, sin):
    first = (_lane() & (HEAD_DIM - 1)) < HEAD_DIM // 2
    partner = jnp.where(first, pltpu.roll(x, LANES - HEAD_DIM // 2, axis=1), pltpu.roll(x, HEAD_DIM // 2, axis=1))
    return x * cos + partner * sin


def _gqa_kernel(q_ref, k_ref, v_ref, cq_ref, sq_ref, ck_ref, sk_ref, gq_ref, gk_ref, o_ref, kproc_ref, *scratch,
                n_chunks):
    qi = pl.program_id(1)
    tq = GQA_TQ

    @pl.when(qi == 0)
    def _():
        kf = _rms_heads(k_ref[...].astype(F32), gk_ref[...])
        kproc_ref[...] = _rope(kf, ck_ref[...], sk_ref[...]).astype(BF16)

    cq, sq, gq = cq_ref[...], sq_ref[...], gq_ref[...]
    parts = []
    for c in range(BRANCH_W // LANES):
        qf = _rms_heads(q_ref[:, c * LANES:(c + 1) * LANES].astype(F32), gq)
        parts.append(_split_heads(_rope(qf, cq, sq) * (HEAD_DIM ** -0.5 * LOG2E)))
    o = _flash(jnp.concatenate(parts, axis=0), kproc_ref, v_ref, scratch, n_chunks, GQA_TK)
    for c in range(BRANCH_W // LANES):
        o_ref[:, c * LANES:(c + 1) * LANES] = _merge_heads(o[2 * c * tq:(2 * c + 2) * tq], tq).astype(o_ref.dtype)


def _gqa_attention(h, cos_t, sin_t, gq, gk, B, T):
    M = h.shape[0]
    tq, tk = GQA_TQ, GQA_TK
    assert T % tk == 0
    nq = T // tq
    vec = pl.BlockSpec((1, LANES), lambda b, qi: (0, 0))
    qtab = pl.BlockSpec((tq, LANES), lambda b, qi: (qi, 0))
    ktab = pl.BlockSpec((T, LANES), lambda b, qi: (0, 0))
    return pl.pallas_call(
        functools.partial(_gqa_kernel, n_chunks=T // tk), name="gqa_attn",
        grid=(B, nq),
        in_specs=[pl.BlockSpec((tq, BRANCH_W), lambda b, qi: (b * nq + qi, COL_QC // BRANCH_W)),
                  pl.BlockSpec((T, LANES), lambda b, qi: (b, COL_KC // LANES)),
                  pl.BlockSpec((T, LANES), lambda b, qi: (b, COL_VC // LANES)),
                  qtab, qtab, ktab, ktab, vec, vec],
        out_specs=pl.BlockSpec((tq, BRANCH_W), lambda b, qi: (b * nq + qi, 0)),
        out_shape=jax.ShapeDtypeStruct((M, BRANCH_W), BF16),
        scratch_shapes=[pltpu.VMEM((T, LANES), BF16)] + _flash_scratch(GQA_HEADS * tq, tk),
        compiler_params=_params(2),
    )(h, h, h, cos_t, sin_t, cos_t, sin_t, gq, gk)


DIL_TQ = 128
DIL_HALF = 64
assert all(w // (2 * d) == DIL_HALF for w, d in DIL_GROUPS)


def _dil_bias_tiles(t5_tab, g, L):
    dil = DIL_GROUPS[g][1]
    tq = DIL_TQ
    W = min(2 * tq, L)
    offs = range(0, -(W - tq) - 1, -DIL_HALF)
    lo = offs[-1] - (tq - 1)
    rel = jnp.arange(lo, W, dtype=jnp.int32)
    s0 = DIFF_HEADS + g * DIL_SLOTS
    by_rel = t5_tab[:, s0:s0 + DIL_SLOTS][_t5_bucket(rel * dil)]
    by_rel = jnp.where((jnp.abs(rel) <= DIL_HALF)[:, None], by_rel, NEG_INF).T
    x = jnp.stack([by_rel[:, d - (tq - 1) - lo:d + W - lo] for d in offs])
    return _toeplitz(x, tq, W).astype(F32)


def _dil_kernel(q_ref, k_ref, v_ref, tiles_ref, o_ref, lse_ref, *, L, W):
    n = pl.program_id(2)
    tq = DIL_TQ
    ws = jnp.clip(n * tq - DIL_HALF, 0, L - W)
    var = (n * tq - ws) // DIL_HALF
    ws = pl.multiple_of(ws, DIL_HALF)
    for c in range(BRANCH_W // LANES):
        cols = slice(c * LANES, (c + 1) * LANES)
        qs = _split_heads(q_ref[:, cols].astype(F32) * HEAD_DIM ** -0.5)
        s = _dot_nt(qs, k_ref[pl.ds(ws, W), cols])
        s = s + jnp.concatenate([tiles_ref[var, 2 * c], tiles_ref[var, 2 * c + 1]], axis=0)
        m = jnp.max(s, axis=-1, keepdims=True)
        p = jnp.exp(s - m)
        l = jnp.sum(p, axis=-1, keepdims=True)
        o = jnp.dot(p.astype(BF16), v_ref[pl.ds(ws, W), cols], preferred_element_type=F32) / l
        lse = jnp.broadcast_to(m + jnp.log(l), (2 * tq, LANES))
        o_ref[:, cols] = _merge_heads(o, tq).astype(o_ref.dtype)
        lse_ref[:, cols] = _merge_heads(lse, tq)


def _dil_attention(hd, col0, tiles, B, T):
    dil, Md, _ = hd.shape
    L = T // dil
    tq = DIL_TQ
    W = min(2 * tq, L)
    assert L % tq == 0
    nq = L // tq
    out_spec = pl.BlockSpec((None, tq, BRANCH_W), lambda b, rho, n: (rho, b * nq + n, 0))
    return pl.pallas_call(
        functools.partial(_dil_kernel, L=L, W=W), name=f"dil_attn_{dil}",
        grid=(B, dil, nq),
        in_specs=[pl.BlockSpec((None, tq, BRANCH_W), lambda b, rho, n: (rho, b * nq + n, col0)),
                  pl.BlockSpec((None, L, BRANCH_W), lambda b, rho, n: (rho, b, col0 + 1)),
                  pl.BlockSpec((None, L, BRANCH_W), lambda b, rho, n: (rho, b, col0 + 2)),
                  pl.BlockSpec(tiles.shape, lambda b, rho, n: (0, 0, 0, 0))],
        out_specs=[out_spec, out_spec],
        out_shape=[jax.ShapeDtypeStruct((dil, Md, BRANCH_W), BF16), jax.ShapeDtypeStruct((dil, Md, BRANCH_W), F32)],
        compiler_params=_params(3),
    )(hd, hd, hd, tiles)


def _dil_merge_kernel(o0_ref, o1_ref, o2_ref, l0_ref, l1_ref, l2_ref, o_ref, *scratch):
    def natural(ref, scr):
        dil, rows, _ = ref.shape
        for c in range(scr.shape[0]):
            for rho in range(dil):
                scr[c, pl.ds(rho, rows, stride=dil), :] = ref[rho, :, c * LANES:(c + 1) * LANES].astype(F32)
        return jnp.concatenate([scr[c] for c in range(scr.shape[0])], axis=1)

    os_ = [o0_ref[0].astype(F32), natural(o1_ref, scratch[0]), natural(o2_ref, scratch[1])]
    ls = [l0_ref[0], natural(l1_ref, scratch[2]), natural(l2_ref, scratch[3])]
    mx = jnp.maximum(jnp.maximum(ls[0], ls[1]), ls[2])
    es = [jnp.exp(l - mx) for l in ls]
    den = es[0] + es[1] + es[2]
    num = es[0] * os_[0] + es[1] * os_[1] + es[2] * os_[2]
    o_ref[...] = (num / den).astype(o_ref.dtype)


def _dil_merge(outs, lses):
    M = outs[0].shape[1]
    tm = _pick(M, 512)

    def spec(a):
        dil = a.shape[0]
        return pl.BlockSpec((dil, tm // dil, BRANCH_W), lambda i: (0, i, 0))

    return pl.pallas_call(
        _dil_merge_kernel, name="dil_merge",
        grid=(M // tm,),
        in_specs=[spec(a) for a in (*outs, *lses)],
        out_specs=pl.BlockSpec((tm, BRANCH_W), lambda i: (i, 0)),
        out_shape=jax.ShapeDtypeStruct((M, BRANCH_W), BF16),
        scratch_shapes=[pltpu.VMEM((BRANCH_W // LANES, tm, LANES), F32)] * 4,
        compiler_params=_params(1),
    )(*outs, *lses)


def _deinterleave(w, n_heads):
    d = w.shape[0]
    return w.reshape(d, n_heads, HEAD_DIM // 2, 2).transpose(0, 1, 3, 2).reshape(d, n_heads * HEAD_DIM)


def _prep_layer(l, w_in, w_gate_down, w_gate_up, w_branch, w_out, w_ffn_in, w_ffn_out, ff_pad):
    d = w_in.shape[1]
    wi = w_in[l]
    qc = _deinterleave(wi[:, 3072:3584], GQA_HEADS).reshape(d, GQA_HEADS, HEAD_DIM)
    qc = qc[:, jnp.array(GQA_HEAD_ORDER)].reshape(d, GQA_HEADS * HEAD_DIM)
    kc = _deinterleave(wi[:, 3584:3712], GQA_KV_HEADS)
    qd, kd, vd = wi[:, 3840:5376], wi[:, 5376:6912], wi[:, 6912:8448]

    def group(g):
        s = slice(g * BRANCH_W, (g + 1) * BRANCH_W)
        return [qd[:, s], kd[:, s], vd[:, s]]

    w_proj = jnp.concatenate([wi[:, :3072], qc, kc, wi[:, 3712:3840], w_gate_down[l]]
                             + group(0) + group(1) + group(2), axis=1)
    assert w_proj.shape[1] == H_MAIN_COLS + 6 * BRANCH_W
    wup = w_gate_up[l].reshape(GATE_RANK, N_BRANCH, d).transpose(1, 0, 2)
    wbr = w_branch[l]
    wbr_c = wbr[2].reshape(GQA_HEADS, HEAD_DIM, d)[jnp.array(GQA_HEAD_ORDER)].reshape(BRANCH_W, d)
    wbr = jnp.stack([wbr[0], wbr[1], wbr_c, wbr[3]])
    ff = w_ffn_out.shape[1]
    wfi = w_ffn_in[l]
    pad = ff_pad - ff
    wfi = jnp.concatenate([jnp.pad(wfi[:, :ff], ((0, 0), (0, pad))), jnp.pad(wfi[:, ff:], ((0, 0), (0, pad)))], axis=1)
    wfo = jnp.pad(w_ffn_out[l], ((0, pad), (0, 0)))
    return dict(w_proj=w_proj.astype(BF16), wup=wup.astype(BF16), wbr=wbr.astype(BF16),
                w_out=w_out[l].astype(BF16), wfi=wfi.astype(BF16), wfo=wfo.astype(BF16))


def _trunk(x, consts, layers, p):
    B, T, D = x.shape
    M = B * T
    depth = len(layers)
    alpha = (2 * depth) ** 0.25
    cos_t, sin_t = _rope_tables(T)
    dil_tiles = [_dil_bias_tiles(p['t5_table'], g, T // dil) for g, (_, dil) in enumerate(DIL_GROUPS)]
    xf, xb = _ln_in(x.reshape(M, D), p['ln_in_g'], p['ln_in_b'])
    for l, lw in enumerate(layers):
        lam_init = 0.8 - 0.6 * math.exp(-0.3 * l)
        h3, hd1, hd2 = _projections(xb, lw['w_proj'], 1024, 512)
        h = h3[0]
        o_a = _na_attention(h, consts['na_bias'][l], B, T)
        lq = p['diff_lambda'][l].astype(F32)
        lam = jnp.exp(jnp.sum(lq[0] * lq[1])) - jnp.exp(jnp.sum(lq[2] * lq[3])) + lam_init
        o_b = _diff_attention(h, consts['diff_tiles'], lam, p['diff_subln_g'][l], lam_init, B, T)
        gq = jnp.tile(jnp.concatenate([p['qk_norm_g'][l, 0, 0::2], p['qk_norm_g'][l, 0, 1::2]]), 2).reshape(1, LANES)
        gk = jnp.tile(jnp.concatenate([p['qk_norm_g'][l, 1, 0::2], p['qk_norm_g'][l, 1, 1::2]]), 2).reshape(1, LANES)
        o_c = _gqa_attention(h, cos_t, sin_t, gq.astype(F32), gk.astype(F32), B, T)
        dil_out = [_dil_attention(h3, COL_D0 // BRANCH_W, dil_tiles[0], B, T),
                   _dil_attention(hd1, 0, dil_tiles[1], B, T),
                   _dil_attention(hd2, 0, dil_tiles[2], B, T)]
        o_d = _dil_merge([o for o, _ in dil_out], [s for _, s in dil_out])
        merged = _gate_merge(h, o_a, o_b, o_c, o_d, lw['wup'], p['b_gate'][l].astype(F32), lw['wbr'], 512, 1024)
        xf, xb = _mm_res_ln(merged, lw['w_out'], xf, p['ln1_g'][l], p['ln1_b'][l], alpha, 512, 512, "out_proj_ln")
        ff = _swiglu(xb, lw['wfi'], 1024, 512)
        xf, xb = _mm_res_ln(ff, lw['wfo'], xf, p['ln2_g'][l], p['ln2_b'][l], alpha, 512, 512, "ffn_out_ln")
    return xf.reshape(B, T, D)


def kernel(x_prompt, x_sample, ln_in_g, ln_in_b, w_in, na_rpb, qk_norm_g, diff_lambda, diff_subln_g, t5_table,
           w_gate_down, w_gate_up, b_gate, w_branch, w_out, ln1_g, ln1_b, w_ffn_in, w_ffn_out, ln2_g, ln2_b):
    depth = w_in.shape[0]
    ff = w_ffn_out.shape[1]
    ff_pad = -(-ff // 1024) * 1024
    layers = [_prep_layer(l, w_in, w_gate_down, w_gate_up, w_branch, w_out, w_ffn_in, w_ffn_out, ff_pad)
              for l in range(depth)]
    consts = dict(na_bias=[_na_bias_table(na_rpb[l]) for l in range(depth)],
                  diff_tiles=_diff_bias_tiles(t5_table[:, :DIFF_HEADS]))
    p = dict(ln_in_g=ln_in_g, ln_in_b=ln_in_b, qk_norm_g=qk_norm_g, diff_lambda=diff_lambda,
             diff_subln_g=diff_subln_g, t5_table=t5_table, b_gate=b_gate, ln1_g=ln1_g, ln1_b=ln1_b,
             ln2_g=ln2_g, ln2_b=ln2_b)
    return (_trunk(x_prompt, consts, layers, p), _trunk(x_sample, consts, layers, p))
```

```python
import functools
import math

import jax
import jax.numpy as jnp
from jax import lax
from jax.experimental import pallas as pl
from jax.experimental.pallas import tpu as pltpu

F32 = jnp.float32
BF16 = jnp.bfloat16

HEAD_DIM = 64
LANES = 128
GRID_W = 64
NA_ROWS = 8
NA_COLS = 16
DIFF_HEADS = 8
DIFF_D = HEAD_DIM // 2
GQA_HEADS = 8
GQA_KV_HEADS = 2
ROPE_AXIS_DIM = HEAD_DIM // 2
ROPE_THETA = 10000.0
DIL_SLOTS = 8
DIL_GROUPS = ((128, 1), (512, 4), (2048, 16))
T5_BUCKETS = 32
T5_MAX_DIST = 128
N_BRANCH = 4
BRANCH_W = 8 * HEAD_DIM
GATE_RANK = 256
LN_EPS = 1e-5
RMS_EPS = 1e-6
NEG_INF = -1e30

COL_QA, COL_KA, COL_VA = 0, 512, 1024
COL_QB, COL_KB, COL_VB = 1536, 2048, 2560
COL_QC = 3072
COL_KC, COL_VC = 3584, 3712
COL_GZ = 3840
COL_D0 = 4096
H_MAIN_COLS = COL_D0 + 3 * BRANCH_W
GQA_HEAD_ORDER = (0, 4, 1, 5, 2, 6, 3, 7)

VMEM_LIMIT = 56 * 1024 * 1024


def _params(n_axes, vmem=VMEM_LIMIT):
    return pltpu.CompilerParams(dimension_semantics=("arbitrary",) * n_axes, vmem_limit_bytes=vmem)


def _pick(n, pref):
    t = min(n, pref)
    while n % t:
        t //= 2
    return t


def _ln_rows(y, g, b):
    mu = jnp.mean(y, axis=-1, keepdims=True)
    yc = y - mu
    var = jnp.mean(yc * yc, axis=-1, keepdims=True)
    return yc * lax.rsqrt(var + LN_EPS) * g + b


def _ln_in_kernel(x_ref, g_ref, b_ref, xo_ref, xb_ref):
    r = _ln_rows(x_ref[...], g_ref[...], b_ref[...])
    xo_ref[...] = r
    xb_ref[...] = r.astype(BF16)


def _ln_in(x, g, b):
    M, D = x.shape
    tm = _pick(M, 256)
    row = pl.BlockSpec((tm, D), lambda i: (i, 0))
    vec = pl.BlockSpec((1, D), lambda i: (0, 0))
    return pl.pallas_call(
        _ln_in_kernel, name="ln_in",
        grid=(M // tm,),
        in_specs=[row, vec, vec],
        out_specs=[row, row],
        out_shape=[jax.ShapeDtypeStruct((M, D), F32), jax.ShapeDtypeStruct((M, D), BF16)],
        compiler_params=_params(1),
    )(x, g.reshape(1, D), b.reshape(1, D))


def _proj_kernel(x_ref, w_ref, om_ref, o1_ref, o2_ref, r_ref, *, n_main, n_dil):
    j = pl.program_id(1)
    r = jnp.dot(x_ref[...], w_ref[...], preferred_element_type=F32)

    def residue_major(o_ref):
        dil, rows, _ = o_ref.shape
        for c in range(r_ref.shape[0]):
            cols = slice(c * LANES, (c + 1) * LANES)
            r_ref[c] = r[:, cols]
            for rho in range(dil):
                o_ref[rho, :, cols] = r_ref[c, pl.ds(rho, rows, stride=dil), :].astype(o_ref.dtype)

    @pl.when(j < n_main)
    def _():
        om_ref[0] = r.astype(om_ref.dtype)

    @pl.when((j >= n_main) & (j < n_main + n_dil))
    def _():
        residue_major(o1_ref)

    @pl.when(j >= n_main + n_dil)
    def _():
        residue_major(o2_ref)


def _projections(x, w, tm, tn):
    M, K = x.shape
    tm, tn = _pick(M, tm), _pick(3 * BRANCH_W, tn)
    n_main, n_dil = H_MAIN_COLS // tn, 3 * BRANCH_W // tn
    d1, d2 = DIL_GROUPS[1][1], DIL_GROUPS[2][1]
    assert w.shape[1] == (n_main + 2 * n_dil) * tn and tm % d2 == 0

    def out_spec(dil, first, count):
        return pl.BlockSpec((dil, tm // dil, tn), lambda i, j: (0, i, jnp.clip(j - first, 0, count - 1)))

    return pl.pallas_call(
        functools.partial(_proj_kernel, n_main=n_main, n_dil=n_dil), name="projections",
        grid=(M // tm, n_main + 2 * n_dil),
        in_specs=[pl.BlockSpec((tm, K), lambda i, j: (i, 0)), pl.BlockSpec((K, tn), lambda i, j: (0, j))],
        out_specs=[out_spec(1, 0, n_main), out_spec(d1, n_main, n_dil), out_spec(d2, n_main + n_dil, n_dil)],
        out_shape=[jax.ShapeDtypeStruct((1, M, H_MAIN_COLS), BF16),
                   jax.ShapeDtypeStruct((d1, M // d1, 3 * BRANCH_W), BF16),
                   jax.ShapeDtypeStruct((d2, M // d2, 3 * BRANCH_W), BF16)],
        scratch_shapes=[pltpu.VMEM((tn // LANES, tm, LANES), F32)],
        compiler_params=_params(2),
    )(x, w)


def _swiglu_kernel(x_ref, wa_ref, wb_ref, o_ref):
    x = x_ref[...]
    a = jnp.dot(x, wa_ref[...], preferred_element_type=F32)
    b = jnp.dot(x, wb_ref[...], preferred_element_type=F32)
    o_ref[...] = (a * (1.0 / (1.0 + jnp.exp(-a))) * b).astype(o_ref.dtype)


def _swiglu(x, w, tm, tn):
    M, K = x.shape
    F = w.shape[1] // 2
    tm, tn = _pick(M, tm), _pick(F, tn)
    nf = F // tn
    return pl.pallas_call(
        _swiglu_kernel, name="swiglu",
        grid=(M // tm, nf),
        in_specs=[pl.BlockSpec((tm, K), lambda i, j: (i, 0)),
                  pl.BlockSpec((K, tn), lambda i, j: (0, j)),
                  pl.BlockSpec((K, tn), lambda i, j: (0, j + nf))],
        out_specs=pl.BlockSpec((tm, tn), lambda i, j: (i, j)),
        out_shape=jax.ShapeDtypeStruct((M, F), BF16),
        compiler_params=_params(2),
    )(x, w, w)


MM_LN_COL_CHUNK = 1024
MM_LN_ROW_CHUNK = 128
MM_LN_UNROLL = 1


def _mm_res_ln_kernel(a_ref, w_ref, xres_ref, g_ref, b_ref, xo_ref, xb_ref, *, nk, alpha):
    k = pl.program_id(1)
    tm, D = xo_ref.shape

    cw = _pick(D, MM_LN_COL_CHUNK)

    def partial_products(accumulate):
        a = a_ref[...]
        for c in range(D // cw):
            cols = slice(c * cw, (c + 1) * cw)
            part = jnp.dot(a, w_ref[:, cols], preferred_element_type=F32)
            xo_ref[:, cols] = xo_ref[:, cols] + part if accumulate else part

    @pl.when(k == 0)
    def _():
        partial_products(False)

    @pl.when(k > 0)
    def _():
        partial_products(True)

    @pl.when(k == nk - 1)
    def _():
        rw = _pick(tm, MM_LN_ROW_CHUNK)

        def ln_chunk(r, carry):
            rows = pl.ds(pl.multiple_of(r * rw, rw), rw)
            res = _ln_rows(alpha * xres_ref[rows, :] + xo_ref[rows, :], g_ref[...], b_ref[...])
            xo_ref[rows, :] = res
            xb_ref[rows, :] = res.astype(BF16)
            return carry

        lax.fori_loop(0, tm // rw, ln_chunk, 0, unroll=MM_LN_UNROLL)


def _mm_res_ln(a, w, xres, g, b, alpha, tm, tk, name):
    M, K = a.shape
    D = w.shape[1]
    tm, tk = _pick(M, tm), _pick(K, tk)
    nk = K // tk
    row = pl.BlockSpec((tm, D), lambda i, k: (i, 0))
    vec = pl.BlockSpec((1, D), lambda i, k: (0, 0))
    return pl.pallas_call(
        functools.partial(_mm_res_ln_kernel, nk=nk, alpha=alpha), name=name,
        grid=(M // tm, nk),
        in_specs=[pl.BlockSpec((tm, tk), lambda i, k: (i, k)),
                  pl.BlockSpec((tk, D), lambda i, k: (k, 0)),
                  row, vec, vec],
        out_specs=[row, row],
        out_shape=[jax.ShapeDtypeStruct((M, D), F32), jax.ShapeDtypeStruct((M, D), BF16)],
        compiler_params=_params(2),
    )(a, w, xres, g.reshape(1, D), b.reshape(1, D))


def _gate_merge_kernel(gz_ref, oa_ref, ob_ref, oc_ref, od_ref, wup_ref, bg_ref, wbr_ref, o_ref):
    gz = gz_ref[...]
    acc = None
    for n, o_n in enumerate((oa_ref, ob_ref, oc_ref, od_ref)):
        z = jnp.dot(gz, wup_ref[n], preferred_element_type=F32) + bg_ref[n:n + 1, :]
        t = jnp.dot(o_n[...], wbr_ref[n], preferred_element_type=F32)
        term = t * (1.0 / (1.0 + jnp.exp(-z)))
        acc = term if acc is None else acc + term
    o_ref[...] = acc.astype(o_ref.dtype)


def _gate_merge(h, o_a, o_b, o_c, o_d, wup, bg, wbr, tm, tn):
    M = h.shape[0]
    D = wup.shape[2]
    tm, tn = _pick(M, tm), _pick(D, tn)
    o_spec = pl.BlockSpec((tm, BRANCH_W), lambda i, j: (i, 0))
    return pl.pallas_call(
        _gate_merge_kernel, name="gate_merge",
        grid=(M // tm, D // tn),
        in_specs=[pl.BlockSpec((tm, GATE_RANK), lambda i, j: (i, COL_GZ // GATE_RANK)),
                  o_spec, o_spec, o_spec, o_spec,
                  pl.BlockSpec((N_BRANCH, GATE_RANK, tn), lambda i, j: (0, 0, j)),
                  pl.BlockSpec((N_BRANCH, tn), lambda i, j: (0, j)),
                  pl.BlockSpec((N_BRANCH, BRANCH_W, tn), lambda i, j: (0, 0, j))],
        out_specs=pl.BlockSpec((tm, tn), lambda i, j: (i, j)),
        out_shape=jax.ShapeDtypeStruct((M, D), BF16),
        compiler_params=_params(2),
    )(h, o_a, o_b, o_c, o_d, wup, bg, wbr)


def _lane():
    return lax.broadcasted_iota(jnp.int32, (1, LANES), 1)


def _dot_nt(a, b):
    return lax.dot_general(a, b, (((1,), (1,)), ((), ())), preferred_element_type=F32)


def _split_heads(x):
    lo = (_lane() < HEAD_DIM).astype(F32)
    return jnp.concatenate([(x * lo).astype(BF16), (x * (1.0 - lo)).astype(BF16)], axis=0)


def _merge_heads(o, rows):
    return jnp.where(_lane() < HEAD_DIM, o[:rows], o[rows:])


def _group_mean(x):
    r = lax.broadcasted_iota(jnp.int32, (LANES, LANES), 0) >> 6
    c = lax.broadcasted_iota(jnp.int32, (LANES, LANES), 1) >> 6
    p = jnp.where(r == c, 1.0 / HEAD_DIM, 0.0).astype(BF16)
    hi = x.astype(BF16)
    lo = (x - hi.astype(F32)).astype(BF16)
    return jnp.dot(hi, p, preferred_element_type=F32) + jnp.dot(lo, p, preferred_element_type=F32)


def _rms_heads(x, g):
    return x * lax.rsqrt(_group_mean(x * x) + RMS_EPS) * g


FLASH_ROWS = 32
LOG2E = math.log2(math.e)


def _flash_scratch(rows, tk):
    return [pltpu.VMEM((rows, tk), F32), pltpu.VMEM((rows, tk), BF16)] + [pltpu.VMEM((rows, LANES), F32)] * 4


def _flash(qs, k_ref, v_ref, scratch, n_chunks, tk, bias_fn=None):
    s_ref, p_ref, m_ref, a_ref, l_ref, acc_ref = scratch
    rows = qs.shape[0]
    rb = FLASH_ROWS
    ncol = tk // LANES
    assert rows % rb == 0
    m_ref[...] = jnp.full(m_ref.shape, NEG_INF, F32)
    l_ref[...] = jnp.zeros(l_ref.shape, F32)
    acc_ref[...] = jnp.zeros(acc_ref.shape, F32)

    def chunk(t, carry):
        ks = pl.multiple_of(t * tk, tk)
        s = _dot_nt(qs, k_ref[pl.ds(ks, tk), :])
        mx = None
        for c in range(ncol):
            sc = s[:, c * LANES:(c + 1) * LANES]
            if bias_fn is not None:
                sc = sc + bias_fn(t, c)
            s_ref[:, c * LANES:(c + 1) * LANES] = sc
            mx = sc if mx is None else jnp.maximum(mx, sc)
        m_old = m_ref[...]
        m_new = jnp.maximum(m_old, jnp.broadcast_to(jnp.max(mx, axis=-1, keepdims=True), (rows, LANES)))
        m_ref[...] = m_new
        a_ref[...] = jnp.exp2(m_old - m_new)
        for i in range(rows // rb):
            rs = slice(i * rb, (i + 1) * rb)
            m_blk = m_ref[rs, :]
            p = [jnp.exp2(s_ref[rs, c * LANES:(c + 1) * LANES] - m_blk) for c in range(ncol)]
            l_ref[rs, :] = a_ref[rs, :] * l_ref[rs, :] + functools.reduce(jnp.add, p)
            for c in range(ncol):
                p_ref[rs, c * LANES:(c + 1) * LANES] = p[c].astype(BF16)
        acc_ref[...] = a_ref[...] * acc_ref[...] + jnp.dot(p_ref[...], v_ref[pl.ds(ks, tk), :],
                                                          preferred_element_type=F32)
        return carry

    lax.fori_loop(0, n_chunks, chunk, 0)
    return acc_ref[...] / jnp.sum(l_ref[...], axis=-1, keepdims=True)


def _t5_bucket(rel):
    half = T5_BUCKETS // 2
    max_exact = half // 2
    n = jnp.abs(rel)
    nf = jnp.maximum(n, 1).astype(jnp.float32)
    large = max_exact + (jnp.log(nf / max_exact) / math.log(T5_MAX_DIST / max_exact)
                         * (half - max_exact)).astype(jnp.int32)
    large = jnp.minimum(large, half - 1)
    return jnp.where(rel > 0, half, 0) + jnp.where(n < max_exact, n, large)


def _toeplitz(x, rows, cols):
    n = rows + cols
    pad = n - x.shape[-1]
    x = x[..., :n] if pad <= 0 else jnp.pad(x, [(0, 0)] * (x.ndim - 1) + [(0, pad)])
    t = jnp.tile(x, (1,) * (x.ndim - 1) + (rows,))[..., :rows * (n - 1)]
    return t.reshape(x.shape[:-1] + (rows, n - 1))[..., rows - 1:rows - 1 + cols]


def _na_bias_table(rpb):
    H = rpb.shape[0]
    qcol = jnp.arange(GRID_W, dtype=jnp.int32)
    kcol = jnp.arange(GRID_W, dtype=jnp.int32)
    c0 = jnp.clip(qcol - NA_COLS // 2, 0, GRID_W - NA_COLS)
    col_ok = (kcol[None, :] >= c0[:, None]) & (kcol[None, :] < c0[:, None] + NA_COLS)
    rel = jnp.arange(-(GRID_W - 1), GRID_W, dtype=jnp.int32)
    by_rel = rpb[:, :, jnp.clip(rel + NA_COLS - 1, 0, 2 * NA_COLS - 2)]
    rpb_col = jnp.where(col_ok, _toeplitz(by_rel, GRID_W, GRID_W), NEG_INF)
    t = jnp.stack([rpb_col[:, NA_ROWS - 1 - d:2 * NA_ROWS - 1 - d] for d in range(NA_ROWS)])
    return t.transpose(0, 1, 3, 2, 4).reshape(NA_ROWS, H, GRID_W, NA_ROWS * GRID_W).astype(F32)


NA_ROWS_PER_STEP = 4


def _na_kernel(q_ref, k_ref, v_ref, bias_ref, o_ref, *, R):
    nk = NA_ROWS * GRID_W
    for rr in range(NA_ROWS_PER_STEP):
        r = pl.program_id(1) * NA_ROWS_PER_STEP + rr
        r0 = jnp.clip(r - NA_ROWS // 2, 0, R - NA_ROWS)
        delta = r - r0
        ks = pl.multiple_of(r0 * GRID_W, GRID_W)
        rows = slice(rr * GRID_W, (rr + 1) * GRID_W)
        for c in range(BRANCH_W // LANES):
            cols = slice(c * LANES, (c + 1) * LANES)
            qs = _split_heads(q_ref[rows, cols].astype(F32) * HEAD_DIM ** -0.5)
            s = _dot_nt(qs, k_ref[pl.ds(ks, nk), cols])
            s = s + jnp.concatenate([bias_ref[delta, 2 * c], bias_ref[delta, 2 * c + 1]], axis=0)
            m = jnp.max(s, axis=-1, keepdims=True)
            p = jnp.exp(s - m)
            l = jnp.sum(p, axis=-1, keepdims=True)
            o = jnp.dot(p.astype(BF16), v_ref[pl.ds(ks, nk), cols], preferred_element_type=F32) / l
            o_ref[rows, cols] = _merge_heads(o, GRID_W).astype(o_ref.dtype)


def _na_attention(h, bias, B, T):
    R = T // GRID_W
    assert R >= NA_ROWS and R % NA_ROWS_PER_STEP == 0
    M = h.shape[0]
    nr = R // NA_ROWS_PER_STEP
    qrows = NA_ROWS_PER_STEP * GRID_W
    return pl.pallas_call(
        functools.partial(_na_kernel, R=R), name="na_attn",
        grid=(B, nr),
        in_specs=[pl.BlockSpec((qrows, BRANCH_W), lambda b, r: (b * nr + r, COL_QA // BRANCH_W)),
                  pl.BlockSpec((T, BRANCH_W), lambda b, r: (b, COL_KA // BRANCH_W)),
                  pl.BlockSpec((T, BRANCH_W), lambda b, r: (b, COL_VA // BRANCH_W)),
                  pl.BlockSpec(bias.shape, lambda b, r: (0, 0, 0, 0))],
        out_specs=pl.BlockSpec((qrows, BRANCH_W), lambda b, r: (b * nr + r, 0)),
        out_shape=jax.ShapeDtypeStruct((M, BRANCH_W), BF16),
        compiler_params=_params(2),
    )(h, h, h, bias)


DIFF_TQ, DIFF_TK = 256, 512
DIFF_D_LO = -(DIFF_TK + DIFF_TQ)
DIFF_D_HI = 2 * DIFF_TQ
assert DIFF_TQ >= T5_MAX_DIST and DIFF_TK % DIFF_TQ == 0


def _diff_bias_tiles(t5_tab):
    tq, tk = DIFF_TQ, DIFF_TK
    lo = DIFF_D_LO - (tq - 1)
    rel = jnp.arange(lo, DIFF_D_HI + tk, dtype=jnp.int32)
    by_rel = t5_tab[_t5_bucket(rel)].T
    offs = range(DIFF_D_LO, DIFF_D_HI + 1, tq)
    x = jnp.stack([by_rel[:, d - (tq - 1) - lo:d + tk - lo] for d in offs], axis=1)
    return (_toeplitz(x, tq, tk) * LOG2E).astype(F32)


def _diff_kernel(lam_ref, q_ref, k_ref, v_ref, tiles_ref, g_ref, o_ref, *scratch, n_chunks, out_scale):
    qi = pl.program_id(2)
    tq, tk = DIFF_TQ, DIFF_TK
    lane = _lane()
    qf = q_ref[...].astype(F32) * (DIFF_D ** -0.5 * LOG2E)
    qs = jnp.concatenate([(qf * ((lane >> 5) == v).astype(F32)).astype(BF16) for v in range(4)], axis=0)

    def bias_fn(kc, c):
        d = jnp.clip(kc * tk - qi * tq, DIFF_D_LO, DIFF_D_HI)
        j = (d - DIFF_D_LO) // tq
        b0 = tiles_ref[0, j, :, c * LANES:(c + 1) * LANES]
        b1 = tiles_ref[1, j, :, c * LANES:(c + 1) * LANES]
        return jnp.concatenate([b0, b0, b1, b1], axis=0)

    o = _flash(qs, k_ref, v_ref, scratch, n_chunks, tk, bias_fn)
    lam = lam_ref[0]
    x = jnp.where(lane < HEAD_DIM, o[:tq] - lam * o[tq:2 * tq], o[2 * tq:3 * tq] - lam * o[3 * tq:])
    o_ref[...] = (_rms_heads(x, g_ref[...]) * out_scale).astype(o_ref.dtype)


def _diff_attention(h, tiles, lam, g, lam_init, B, T):
    M = h.shape[0]
    tq, tk = DIFF_TQ, DIFF_TK
    assert T % tk == 0
    nq = T // tq
    qb, kb, vb = COL_QB // LANES, COL_KB // LANES, COL_VB // LANES
    g2 = jnp.tile(g.astype(F32), 2).reshape(1, LANES)
    return pl.pallas_call(
        functools.partial(_diff_kernel, n_chunks=T // tk, out_scale=1.0 - lam_init), name="diff_attn",
        grid=(B, DIFF_HEADS // 2, nq),
        in_specs=[pl.BlockSpec(memory_space=pltpu.SMEM),
                  pl.BlockSpec((tq, LANES), lambda b, hp, qi: (b * nq + qi, qb + hp)),
                  pl.BlockSpec((T, LANES), lambda b, hp, qi: (b, kb + hp)),
                  pl.BlockSpec((T, LANES), lambda b, hp, qi: (b, vb + hp)),
                  pl.BlockSpec((2,) + tiles.shape[1:], lambda b, hp, qi: (hp, 0, 0, 0)),
                  pl.BlockSpec((1, LANES), lambda b, hp, qi: (0, 0))],
        out_specs=pl.BlockSpec((tq, LANES), lambda b, hp, qi: (b * nq + qi, hp)),
        out_shape=jax.ShapeDtypeStruct((M, BRANCH_W), BF16),
        scratch_shapes=_flash_scratch(4 * tq, tk),
        compiler_params=_params(3),
    )(lam.reshape(1).astype(F32), h, h, h, tiles, g2)


GQA_TQ, GQA_TK = 128, 512


def _rope_tables(T):
    t = jnp.arange(T, dtype=jnp.int32)
    row = (t // GRID_W).astype(jnp.float32)
    col = (t % GRID_W).astype(jnp.float32)
    freqs = ROPE_THETA ** (-jnp.arange(0, ROPE_AXIS_DIM, 2, dtype=jnp.float32) / ROPE_AXIS_DIM)
    ang = jnp.concatenate([row[:, None] * freqs, col[:, None] * freqs], axis=-1)
    cos, sin = jnp.cos(ang), jnp.sin(ang)
    cos_t = jnp.tile(cos, (1, 4))
    sin_t = jnp.tile(jnp.concatenate([-sin, sin], axis=-1), (1, 2))
    return cos_t.astype(F32), sin_t.astype(F32)


def _rope(x, cos, sin):
    first = (_lane() & (HEAD_DIM - 1)) < HEAD_DIM // 2
    partner = jnp.where(first, pltpu.roll(x, LANES - HEAD_DIM // 2, axis=1), pltpu.roll(x, HEAD_DIM // 2, axis=1))
    return x * cos + partner * sin


def _gqa_kernel(q_ref, k_ref, v_ref, cq_ref, sq_ref, ck_ref, sk_ref, gq_ref, gk_ref, o_ref, kproc_ref, *scratch,
                n_chunks):
    qi = pl.program_id(1)
    tq = GQA_TQ

    @pl.when(qi == 0)
    def _():
        kf = _rms_heads(k_ref[...].astype(F32), gk_ref[...])
        kproc_ref[...] = _rope(kf, ck_ref[...], sk_ref[...]).astype(BF16)

    cq, sq, gq = cq_ref[...], sq_ref[...], gq_ref[...]
    parts = []
    for c in range(BRANCH_W // LANES):
        qf = _rms_heads(q_ref[:, c * LANES:(c + 1) * LANES].astype(F32), gq)
        parts.append(_split_heads(_rope(qf, cq, sq) * (HEAD_DIM ** -0.5 * LOG2E)))
    o = _flash(jnp.concatenate(parts, axis=0), kproc_ref, v_ref, scratch, n_chunks, GQA_TK)
    for c in range(BRANCH_W // LANES):
        o_ref[:, c * LANES:(c + 1) * LANES] = _merge_heads(o[2 * c * tq:(2 * c + 2) * tq], tq).astype(o_ref.dtype)


def _gqa_attention(h, cos_t, sin_t, gq, gk, B, T):
    M = h.shape[0]
    tq, tk = GQA_TQ, GQA_TK
    assert T % tk == 0
    nq = T // tq
    vec = pl.BlockSpec((1, LANES), lambda b, qi: (0, 0))
    qtab = pl.BlockSpec((tq, LANES), lambda b, qi: (qi, 0))
    ktab = pl.BlockSpec((T, LANES), lambda b, qi: (0, 0))
    return pl.pallas_call(
        functools.partial(_gqa_kernel, n_chunks=T // tk), name="gqa_attn",
        grid=(B, nq),
        in_specs=[pl.BlockSpec((tq, BRANCH_W), lambda b, qi: (b * nq + qi, COL_QC // BRANCH_W)),
                  pl.BlockSpec((T, LANES), lambda b, qi: (b, COL_KC // LANES)),
                  pl.BlockSpec((T, LANES), lambda b, qi: (b, COL_VC // LANES)),
                  qtab, qtab, ktab, ktab, vec, vec],
        out_specs=pl.BlockSpec((tq, BRANCH_W), lambda b, qi: (b * nq + qi, 0)),
        out_shape=jax.ShapeDtypeStruct((M, BRANCH_W), BF16),
        scratch_shapes=[pltpu.VMEM((T, LANES), BF16)] + _flash_scratch(GQA_HEADS * tq, tk),
        compiler_params=_params(2),
    )(h, h, h, cos_t, sin_t, cos_t, sin_t, gq, gk)


DIL_TQ = 128
DIL_HALF = 64
assert all(w // (2 * d) == DIL_HALF for w, d in DIL_GROUPS)


def _dil_bias_tiles(t5_tab, g, L):
    dil = DIL_GROUPS[g][1]
    tq = DIL_TQ
    W = min(2 * tq, L)
    offs = range(0, -(W - tq) - 1, -DIL_HALF)
    lo = offs[-1] - (tq - 1)
    rel = jnp.arange(lo, W, dtype=jnp.int32)
    s0 = DIFF_HEADS + g * DIL_SLOTS
    by_rel = t5_tab[:, s0:s0 + DIL_SLOTS][_t5_bucket(rel * dil)]
    by_rel = jnp.where((jnp.abs(rel) <= DIL_HALF)[:, None], by_rel, NEG_INF).T
    x = jnp.stack([by_rel[:, d - (tq - 1) - lo:d + W - lo] for d in offs])
    return _toeplitz(x, tq, W).astype(F32)


DIL_TILES_PER_STEP = 4


def _dil_kernel(q_ref, k_ref, v_ref, tiles_ref, o_ref, lse_ref, *, L, W):
    tq = DIL_TQ
    for tt in range(q_ref.shape[0] // tq):
        n = pl.program_id(2) * (q_ref.shape[0] // tq) + tt
        ws = jnp.clip(n * tq - DIL_HALF, 0, L - W)
        var = (n * tq - ws) // DIL_HALF
        ws = pl.multiple_of(ws, DIL_HALF)
        rows = slice(tt * tq, (tt + 1) * tq)
        for c in range(BRANCH_W // LANES):
            cols = slice(c * LANES, (c + 1) * LANES)
            qs = _split_heads(q_ref[rows, cols].astype(F32) * HEAD_DIM ** -0.5)
            s = _dot_nt(qs, k_ref[pl.ds(ws, W), cols])
            s = s + jnp.concatenate([tiles_ref[var, 2 * c], tiles_ref[var, 2 * c + 1]], axis=0)
            m = jnp.max(s, axis=-1, keepdims=True)
            p = jnp.exp(s - m)
            l = jnp.sum(p, axis=-1, keepdims=True)
            o = jnp.dot(p.astype(BF16), v_ref[pl.ds(ws, W), cols], preferred_element_type=F32) / l
            lse = jnp.broadcast_to(m + jnp.log(l), (2 * tq, LANES))
            o_ref[rows, cols] = _merge_heads(o, tq).astype(o_ref.dtype)
            lse_ref[rows, cols] = _merge_heads(lse, tq)


def _dil_attention(hd, col0, tiles, B, T):
    dil, Md, _ = hd.shape
    L = T // dil
    tq = DIL_TQ
    W = min(2 * tq, L)
    assert L % tq == 0
    tps = _pick(L // tq, DIL_TILES_PER_STEP)
    nq = L // (tq * tps)
    out_spec = pl.BlockSpec((None, tq * tps, BRANCH_W), lambda b, rho, n: (rho, b * nq + n, 0))
    return pl.pallas_call(
        functools.partial(_dil_kernel, L=L, W=W), name=f"dil_attn_{dil}",
        grid=(B, dil, nq),
        in_specs=[pl.BlockSpec((None, tq * tps, BRANCH_W), lambda b, rho, n: (rho, b * nq + n, col0)),
                  pl.BlockSpec((None, L, BRANCH_W), lambda b, rho, n: (rho, b, col0 + 1)),
                  pl.BlockSpec((None, L, BRANCH_W), lambda b, rho, n: (rho, b, col0 + 2)),
                  pl.BlockSpec(tiles.shape, lambda b, rho, n: (0, 0, 0, 0))],
        out_specs=[out_spec, out_spec],
        out_shape=[jax.ShapeDtypeStruct((dil, Md, BRANCH_W), BF16), jax.ShapeDtypeStruct((dil, Md, BRANCH_W), F32)],
        compiler_params=_params(3),
    )(hd, hd, hd, tiles)


def _dil_merge_kernel(o0_ref, o1_ref, o2_ref, l0_ref, l1_ref, l2_ref, o_ref, *scratch):
    def natural(ref, scr):
        dil, rows, _ = ref.shape
        for c in range(scr.shape[0]):
            for rho in range(dil):
                scr[c, pl.ds(rho, rows, stride=dil), :] = ref[rho, :, c * LANES:(c + 1) * LANES].astype(F32)
        return jnp.concatenate([scr[c] for c in range(scr.shape[0])], axis=1)

    os_ = [o0_ref[0].astype(F32), natural(o1_ref, scratch[0]), natural(o2_ref, scratch[1])]
    ls = [l0_ref[0], natural(l1_ref, scratch[2]), natural(l2_ref, scratch[3])]
    mx = jnp.maximum(jnp.maximum(ls[0], ls[1]), ls[2])
    es = [jnp.exp(l - mx) for l in ls]
    den = es[0] + es[1] + es[2]
    num = es[0] * os_[0] + es[1] * os_[1] + es[2] * os_[2]
    o_ref[...] = (num / den).astype(o_ref.dtype)


def _dil_merge(outs, lses):
    M = outs[0].shape[1]
    tm = _pick(M, 512)

    def spec(a):
        dil = a.shape[0]
        return pl.BlockSpec((dil, tm // dil, BRANCH_W), lambda i: (0, i, 0))

    return pl.pallas_call(
        _dil_merge_kernel, name="dil_merge",
        grid=(M // tm,),
        in_specs=[spec(a) for a in (*outs, *lses)],
        out_specs=pl.BlockSpec((tm, BRANCH_W), lambda i: (i, 0)),
        out_shape=jax.ShapeDtypeStruct((M, BRANCH_W), BF16),
        scratch_shapes=[pltpu.VMEM((BRANCH_W // LANES, tm, LANES), F32)] * 4,
        compiler_params=_params(1),
    )(*outs, *lses)


def _deinterleave(w, n_heads):
    d = w.shape[0]
    return w.reshape(d, n_heads, HEAD_DIM // 2, 2).transpose(0, 1, 3, 2).reshape(d, n_heads * HEAD_DIM)


def _prep_layer(l, w_in, w_gate_down, w_gate_up, w_branch, w_out, w_ffn_in, w_ffn_out, ff_pad):
    d = w_in.shape[1]
    wi = w_in[l]
    qc = _deinterleave(wi[:, 3072:3584], GQA_HEADS).reshape(d, GQA_HEADS, HEAD_DIM)
    qc = qc[:, jnp.array(GQA_HEAD_ORDER)].reshape(d, GQA_HEADS * HEAD_DIM)
    kc = _deinterleave(wi[:, 3584:3712], GQA_KV_HEADS)
    qd, kd, vd = wi[:, 3840:5376], wi[:, 5376:6912], wi[:, 6912:8448]

    def group(g):
        s = slice(g * BRANCH_W, (g + 1) * BRANCH_W)
        return [qd[:, s], kd[:, s], vd[:, s]]

    w_proj = jnp.concatenate([wi[:, :3072], qc, kc, wi[:, 3712:3840], w_gate_down[l]]
                             + group(0) + group(1) + group(2), axis=1)
    assert w_proj.shape[1] == H_MAIN_COLS + 6 * BRANCH_W
    wup = w_gate_up[l].reshape(GATE_RANK, N_BRANCH, d).transpose(1, 0, 2)
    wbr = w_branch[l]
    wbr_c = wbr[2].reshape(GQA_HEADS, HEAD_DIM, d)[jnp.array(GQA_HEAD_ORDER)].reshape(BRANCH_W, d)
    wbr = jnp.stack([wbr[0], wbr[1], wbr_c, wbr[3]])
    ff = w_ffn_out.shape[1]
    wfi = w_ffn_in[l]
    pad = ff_pad - ff
    wfi = jnp.concatenate([jnp.pad(wfi[:, :ff], ((0, 0), (0, pad))), jnp.pad(wfi[:, ff:], ((0, 0), (0, pad)))], axis=1)
    wfo = jnp.pad(w_ffn_out[l], ((0, pad), (0, 0)))
    return dict(w_proj=w_proj.astype(BF16), wup=wup.astype(BF16), wbr=wbr.astype(BF16),
                w_out=w_out[l].astype(BF16), wfi=wfi.astype(BF16), wfo=wfo.astype(BF16))


def _trunk(x, consts, layers, p):
    B, T, D = x.shape
    M = B * T
    depth = len(layers)
    alpha = (2 * depth) ** 0.25
    cos_t, sin_t = _rope_tables(T)
    dil_tiles = [_dil_bias_tiles(p['t5_table'], g, T // dil) for g, (_, dil) in enumerate(DIL_GROUPS)]
    xf, xb = _ln_in(x.reshape(M, D), p['ln_in_g'], p['ln_in_b'])
    for l, lw in enumerate(layers):
        lam_init = 0.8 - 0.6 * math.exp(-0.3 * l)
        h3, hd1, hd2 = _projections(xb, lw['w_proj'], 1024, 512)
        h = h3[0]
        o_a = _na_attention(h, consts['na_bias'][l], B, T)
        lq = p['diff_lambda'][l].astype(F32)
        lam = jnp.exp(jnp.sum(lq[0] * lq[1])) - jnp.exp(jnp.sum(lq[2] * lq[3])) + lam_init
        o_b = _diff_attention(h, consts['diff_tiles'], lam, p['diff_subln_g'][l], lam_init, B, T)
        gq = jnp.tile(jnp.concatenate([p['qk_norm_g'][l, 0, 0::2], p['qk_norm_g'][l, 0, 1::2]]), 2).reshape(1, LANES)
        gk = jnp.tile(jnp.concatenate([p['qk_norm_g'][l, 1, 0::2], p['qk_norm_g'][l, 1, 1::2]]), 2).reshape(1, LANES)
        o_c = _gqa_attention(h, cos_t, sin_t, gq.astype(F32), gk.astype(F32), B, T)
        dil_out = [_dil_attention(h3, COL_D0 // BRANCH_W, dil_tiles[0], B, T),
                   _dil_attention(hd1, 0, dil_tiles[1], B, T),
                   _dil_attention(hd2, 0, dil_tiles[2], B, T)]
        o_d = _dil_merge([o for o, _ in dil_out], [s for _, s in dil_out])
        merged = _gate_merge(h, o_a, o_b, o_c, o_d, lw['wup'], p['b_gate'][l].astype(F32), lw['wbr'], 1024, 512)
        xf, xb = _mm_res_ln(merged, lw['w_out'], xf, p['ln1_g'][l], p['ln1_b'][l], alpha, 512, 512, "out_proj_ln")
        ff = _swiglu(xb, lw['wfi'], 1024, 512)
        xf, xb = _mm_res_ln(ff, lw['wfo'], xf, p['ln2_g'][l], p['ln2_b'][l], alpha, 512, 512, "ffn_out_ln")
    return xf.reshape(B, T, D)


def kernel(x_prompt, x_sample, ln_in_g, ln_in_b, w_in, na_rpb, qk_norm_g, diff_lambda, diff_subln_g, t5_table,
           w_gate_down, w_gate_up, b_gate, w_branch, w_out, ln1_g, ln1_b, w_ffn_in, w_ffn_out, ln2_g, ln2_b):
    depth = w_in.shape[0]
    ff = w_ffn_out.shape[1]
    ff_pad = -(-ff // 1024) * 1024
    layers = [_prep_layer(l, w_in, w_gate_down, w_gate_up, w_branch, w_out, w_ffn_in, w_ffn_out, ff_pad)
              for l in range(depth)]
    consts = dict(na_bias=[_na_bias_table(na_rpb[l]) for l in range(depth)],
                  diff_tiles=_diff_bias_tiles(t5_table[:, :DIFF_HEADS]))
    p = dict(ln_in_g=ln_in_g, ln_in_b=ln_in_b, qk_norm_g=qk_norm_g, diff_lambda=diff_lambda,
             diff_subln_g=diff_subln_g, t5_table=t5_table, b_gate=b_gate, ln1_g=ln1_g, ln1_b=ln1_b,
             ln2_g=ln2_g, ln2_b=ln2_b)
    return (_trunk(x_prompt, consts, layers, p), _trunk(x_sample, consts, layers, p))
```

```python
import functools
import math

import jax
import jax.numpy as jnp
from jax import lax
from jax.experimental import pallas as pl
from jax.experimental.pallas import tpu as pltpu

F32 = jnp.float32
BF16 = jnp.bfloat16

HEAD_DIM = 64
LANES = 128
GRID_W = 64
NA_ROWS = 8
NA_COLS = 16
DIFF_HEADS = 8
DIFF_D = HEAD_DIM // 2
GQA_HEADS = 8
GQA_KV_HEADS = 2
ROPE_AXIS_DIM = HEAD_DIM // 2
ROPE_THETA = 10000.0
DIL_SLOTS = 8
DIL_GROUPS = ((128, 1), (512, 4), (2048, 16))
T5_BUCKETS = 32
T5_MAX_DIST = 128
N_BRANCH = 4
BRANCH_W = 8 * HEAD_DIM
GATE_RANK = 256
LN_EPS = 1e-5
RMS_EPS = 1e-6
NEG_INF = -1e30

COL_QA, COL_KA, COL_VA = 0, 512, 1024
COL_QB, COL_KB, COL_VB = 1536, 2048, 2560
COL_QC = 3072
COL_KC, COL_VC = 3584, 3712
COL_GZ = 3840
COL_D0 = 4096
H_MAIN_COLS = COL_D0 + 3 * BRANCH_W
GQA_HEAD_ORDER = (0, 4, 1, 5, 2, 6, 3, 7)

VMEM_LIMIT = 56 * 1024 * 1024


def _params(n_axes, vmem=VMEM_LIMIT):
    return pltpu.CompilerParams(dimension_semantics=("arbitrary",) * n_axes, vmem_limit_bytes=vmem)


def _pick(n, pref):
    t = min(n, pref)
    while n % t:
        t //= 2
    return t


def _ln_rows(y, g, b):
    mu = jnp.mean(y, axis=-1, keepdims=True)
    yc = y - mu
    var = jnp.mean(yc * yc, axis=-1, keepdims=True)
    return yc * lax.rsqrt(var + LN_EPS) * g + b


def _ln_in_kernel(x_ref, g_ref, b_ref, xo_ref, xb_ref):
    r = _ln_rows(x_ref[...], g_ref[...], b_ref[...])
    xo_ref[...] = r
    xb_ref[...] = r.astype(BF16)


def _ln_in(x, g, b):
    M, D = x.shape
    tm = _pick(M, 256)
    row = pl.BlockSpec((tm, D), lambda i: (i, 0))
    vec = pl.BlockSpec((1, D), lambda i: (0, 0))
    return pl.pallas_call(
        _ln_in_kernel, name="ln_in",
        grid=(M // tm,),
        in_specs=[row, vec, vec],
        out_specs=[row, row],
        out_shape=[jax.ShapeDtypeStruct((M, D), F32), jax.ShapeDtypeStruct((M, D), BF16)],
        compiler_params=_params(1),
    )(x, g.reshape(1, D), b.reshape(1, D))


def _proj_kernel(x_ref, w_ref, om_ref, o1_ref, o2_ref, r_ref, *, n_main, n_dil):
    j = pl.program_id(1)
    r = jnp.dot(x_ref[...], w_ref[...], preferred_element_type=F32)

    def residue_major(o_ref):
        dil, rows, _ = o_ref.shape
        for c in range(r_ref.shape[0]):
            cols = slice(c * LANES, (c + 1) * LANES)
            r_ref[c] = r[:, cols]
            for rho in range(dil):
                o_ref[rho, :, cols] = r_ref[c, pl.ds(rho, rows, stride=dil), :].astype(o_ref.dtype)

    @pl.when(j < n_main)
    def _():
        om_ref[0] = r.astype(om_ref.dtype)

    @pl.when((j >= n_main) & (j < n_main + n_dil))
    def _():
        residue_major(o1_ref)

    @pl.when(j >= n_main + n_dil)
    def _():
        residue_major(o2_ref)


def _projections(x, w, tm, tn):
    M, K = x.shape
    tm, tn = _pick(M, tm), _pick(3 * BRANCH_W, tn)
    n_main, n_dil = H_MAIN_COLS // tn, 3 * BRANCH_W // tn
    d1, d2 = DIL_GROUPS[1][1], DIL_GROUPS[2][1]
    assert w.shape[1] == (n_main + 2 * n_dil) * tn and tm % d2 == 0

    def out_spec(dil, first, count):
        return pl.BlockSpec((dil, tm // dil, tn), lambda i, j: (0, i, jnp.clip(j - first, 0, count - 1)))

    return pl.pallas_call(
        functools.partial(_proj_kernel, n_main=n_main, n_dil=n_dil), name="projections",
        grid=(M // tm, n_main + 2 * n_dil),
        in_specs=[pl.BlockSpec((tm, K), lambda i, j: (i, 0)), pl.BlockSpec((K, tn), lambda i, j: (0, j))],
        out_specs=[out_spec(1, 0, n_main), out_spec(d1, n_main, n_dil), out_spec(d2, n_main + n_dil, n_dil)],
        out_shape=[jax.ShapeDtypeStruct((1, M, H_MAIN_COLS), BF16),
                   jax.ShapeDtypeStruct((d1, M // d1, 3 * BRANCH_W), BF16),
                   jax.ShapeDtypeStruct((d2, M // d2, 3 * BRANCH_W), BF16)],
        scratch_shapes=[pltpu.VMEM((tn // LANES, tm, LANES), F32)],
        compiler_params=_params(2),
    )(x, w)


def _swiglu_kernel(x_ref, wa_ref, wb_ref, o_ref):
    x = x_ref[...]
    a = jnp.dot(x, wa_ref[...], preferred_element_type=F32)
    b = jnp.dot(x, wb_ref[...], preferred_element_type=F32)
    o_ref[...] = (a * (1.0 / (1.0 + jnp.exp(-a))) * b).astype(o_ref.dtype)


def _swiglu(x, w, tm, tn):
    M, K = x.shape
    F = w.shape[1] // 2
    tm, tn = _pick(M, tm), _pick(F, tn)
    nf = F // tn
    return pl.pallas_call(
        _swiglu_kernel, name="swiglu",
        grid=(M // tm, nf),
        in_specs=[pl.BlockSpec((tm, K), lambda i, j: (i, 0)),
                  pl.BlockSpec((K, tn), lambda i, j: (0, j)),
                  pl.BlockSpec((K, tn), lambda i, j: (0, j + nf))],
        out_specs=pl.BlockSpec((tm, tn), lambda i, j: (i, j)),
        out_shape=jax.ShapeDtypeStruct((M, F), BF16),
        compiler_params=_params(2),
    )(x, w, w)


MM_LN_COL_CHUNK = 1024
MM_LN_ROW_CHUNK = 128
MM_LN_UNROLL = 1


def _mm_res_ln_kernel(a_ref, w_ref, xres_ref, g_ref, b_ref, xo_ref, xb_ref, *, nk, alpha):
    k = pl.program_id(1)
    tm, D = xo_ref.shape

    cw = _pick(D, MM_LN_COL_CHUNK)

    def partial_products(accumulate):
        a = a_ref[...]
        for c in range(D // cw):
            cols = slice(c * cw, (c + 1) * cw)
            part = jnp.dot(a, w_ref[:, cols], preferred_element_type=F32)
            xo_ref[:, cols] = xo_ref[:, cols] + part if accumulate else part

    @pl.when(k == 0)
    def _():
        partial_products(False)

    @pl.when(k > 0)
    def _():
        partial_products(True)

    @pl.when(k == nk - 1)
    def _():
        rw = _pick(tm, MM_LN_ROW_CHUNK)

        def ln_chunk(r, carry):
            rows = pl.ds(pl.multiple_of(r * rw, rw), rw)
            res = _ln_rows(alpha * xres_ref[rows, :] + xo_ref[rows, :], g_ref[...], b_ref[...])
            xo_ref[rows, :] = res
            xb_ref[rows, :] = res.astype(BF16)
            return carry

        lax.fori_loop(0, tm // rw, ln_chunk, 0, unroll=MM_LN_UNROLL)


def _mm_res_ln(a, w, xres, g, b, alpha, tm, tk, name):
    M, K = a.shape
    D = w.shape[1]
    tm, tk = _pick(M, tm), _pick(K, tk)
    nk = K // tk
    row = pl.BlockSpec((tm, D), lambda i, k: (i, 0))
    vec = pl.BlockSpec((1, D), lambda i, k: (0, 0))
    return pl.pallas_call(
        functools.partial(_mm_res_ln_kernel, nk=nk, alpha=alpha), name=name,
        grid=(M // tm, nk),
        in_specs=[pl.BlockSpec((tm, tk), lambda i, k: (i, k)),
                  pl.BlockSpec((tk, D), lambda i, k: (k, 0)),
                  row, vec, vec],
        out_specs=[row, row],
        out_shape=[jax.ShapeDtypeStruct((M, D), F32), jax.ShapeDtypeStruct((M, D), BF16)],
        compiler_params=_params(2),
    )(a, w, xres, g.reshape(1, D), b.reshape(1, D))


def _gate_merge_kernel(gz_ref, oa_ref, ob_ref, oc_ref, od_ref, wup_ref, bg_ref, wbr_ref, o_ref):
    gz = gz_ref[...]
    acc = None
    for n, o_n in enumerate((oa_ref, ob_ref, oc_ref, od_ref)):
        z = jnp.dot(gz, wup_ref[n], preferred_element_type=F32) + bg_ref[n:n + 1, :]
        t = jnp.dot(o_n[...], wbr_ref[n], preferred_element_type=F32)
        term = t * (1.0 / (1.0 + jnp.exp(-z)))
        acc = term if acc is None else acc + term
    o_ref[...] = acc.astype(o_ref.dtype)


def _gate_merge(h, o_a, o_b, o_c, o_d, wup, bg, wbr, tm, tn):
    M = h.shape[0]
    D = wup.shape[2]
    tm, tn = _pick(M, tm), _pick(D, tn)
    o_spec = pl.BlockSpec((tm, BRANCH_W), lambda i, j: (i, 0))
    return pl.pallas_call(
        _gate_merge_kernel, name="gate_merge",
        grid=(M // tm, D // tn),
        in_specs=[pl.BlockSpec((tm, GATE_RANK), lambda i, j: (i, COL_GZ // GATE_RANK)),
                  o_spec, o_spec, o_spec, o_spec,
                  pl.BlockSpec((N_BRANCH, GATE_RANK, tn), lambda i, j: (0, 0, j)),
                  pl.BlockSpec((N_BRANCH, tn), lambda i, j: (0, j)),
                  pl.BlockSpec((N_BRANCH, BRANCH_W, tn), lambda i, j: (0, 0, j))],
        out_specs=pl.BlockSpec((tm, tn), lambda i, j: (i, j)),
        out_shape=jax.ShapeDtypeStruct((M, D), BF16),
        compiler_params=_params(2),
    )(h, o_a, o_b, o_c, o_d, wup, bg, wbr)


def _lane():
    return lax.broadcasted_iota(jnp.int32, (1, LANES), 1)


def _dot_nt(a, b):
    return lax.dot_general(a, b, (((1,), (1,)), ((), ())), preferred_element_type=F32)


def _split_heads(x):
    lo = (_lane() < HEAD_DIM).astype(F32)
    return jnp.concatenate([(x * lo).astype(BF16), (x * (1.0 - lo)).astype(BF16)], axis=0)


def _merge_heads(o, rows):
    return jnp.where(_lane() < HEAD_DIM, o[:rows], o[rows:])


def _group_mean(x):
    r = lax.broadcasted_iota(jnp.int32, (LANES, LANES), 0) >> 6
    c = lax.broadcasted_iota(jnp.int32, (LANES, LANES), 1) >> 6
    p = jnp.where(r == c, 1.0 / HEAD_DIM, 0.0).astype(BF16)
    hi = x.astype(BF16)
    lo = (x - hi.astype(F32)).astype(BF16)
    return jnp.dot(hi, p, preferred_element_type=F32) + jnp.dot(lo, p, preferred_element_type=F32)


def _rms_heads(x, g):
    return x * lax.rsqrt(_group_mean(x * x) + RMS_EPS) * g


FLASH_ROWS = 16
LOG2E = math.log2(math.e)


def _flash_scratch(rows, tk):
    return [pltpu.VMEM((rows, tk), F32), pltpu.VMEM((rows, tk), BF16)] + [pltpu.VMEM((rows, LANES), F32)] * 4


def _flash(qs, k_ref, v_ref, scratch, n_chunks, tk, bias_fn=None):
    s_ref, p_ref, m_ref, a_ref, l_ref, acc_ref = scratch
    rows = qs.shape[0]
    rb = FLASH_ROWS
    ncol = tk // LANES
    assert rows % rb == 0
    m_ref[...] = jnp.full(m_ref.shape, NEG_INF, F32)
    l_ref[...] = jnp.zeros(l_ref.shape, F32)
    acc_ref[...] = jnp.zeros(acc_ref.shape, F32)

    def chunk(t, carry):
        ks = pl.multiple_of(t * tk, tk)
        s = _dot_nt(qs, k_ref[pl.ds(ks, tk), :])
        mx = None
        for c in range(ncol):
            sc = s[:, c * LANES:(c + 1) * LANES]
            if bias_fn is not None:
                sc = sc + bias_fn(t, c)
            s_ref[:, c * LANES:(c + 1) * LANES] = sc
            mx = sc if mx is None else jnp.maximum(mx, sc)
        m_old = m_ref[...]
        m_new = jnp.maximum(m_old, jnp.broadcast_to(jnp.max(mx, axis=-1, keepdims=True), (rows, LANES)))
        m_ref[...] = m_new
        a_ref[...] = jnp.exp2(m_old - m_new)
        for i in range(rows // rb):
            rs = slice(i * rb, (i + 1) * rb)
            m_blk = m_ref[rs, :]
            p = [jnp.exp2(s_ref[rs, c * LANES:(c + 1) * LANES] - m_blk) for c in range(ncol)]
            l_ref[rs, :] = a_ref[rs, :] * l_ref[rs, :] + functools.reduce(jnp.add, p)
            for c in range(ncol):
                p_ref[rs, c * LANES:(c + 1) * LANES] = p[c].astype(BF16)
        acc_ref[...] = a_ref[...] * acc_ref[...] + jnp.dot(p_ref[...], v_ref[pl.ds(ks, tk), :],
                                                          preferred_element_type=F32)
        return carry

    lax.fori_loop(0, n_chunks, chunk, 0)
    return acc_ref[...] / jnp.sum(l_ref[...], axis=-1, keepdims=True)


def _t5_bucket(rel):
    half = T5_BUCKETS // 2
    max_exact = half // 2
    n = jnp.abs(rel)
    nf = jnp.maximum(n, 1).astype(jnp.float32)
    large = max_exact + (jnp.log(nf / max_exact) / math.log(T5_MAX_DIST / max_exact)
                         * (half - max_exact)).astype(jnp.int32)
    large = jnp.minimum(large, half - 1)
    return jnp.where(rel > 0, half, 0) + jnp.where(n < max_exact, n, large)


def _toeplitz(x, rows, cols):
    n = rows + cols
    pad = n - x.shape[-1]
    x = x[..., :n] if pad <= 0 else jnp.pad(x, [(0, 0)] * (x.ndim - 1) + [(0, pad)])
    t = jnp.tile(x, (1,) * (x.ndim - 1) + (rows,))[..., :rows * (n - 1)]
    return t.reshape(x.shape[:-1] + (rows, n - 1))[..., rows - 1:rows - 1 + cols]


def _na_bias_table(rpb):
    H = rpb.shape[0]
    qcol = jnp.arange(GRID_W, dtype=jnp.int32)
    kcol = jnp.arange(GRID_W, dtype=jnp.int32)
    c0 = jnp.clip(qcol - NA_COLS // 2, 0, GRID_W - NA_COLS)
    col_ok = (kcol[None, :] >= c0[:, None]) & (kcol[None, :] < c0[:, None] + NA_COLS)
    rel = jnp.arange(-(GRID_W - 1), GRID_W, dtype=jnp.int32)
    by_rel = rpb[:, :, jnp.clip(rel + NA_COLS - 1, 0, 2 * NA_COLS - 2)]
    rpb_col = jnp.where(col_ok, _toeplitz(by_rel, GRID_W, GRID_W), NEG_INF)
    t = jnp.stack([rpb_col[:, NA_ROWS - 1 - d:2 * NA_ROWS - 1 - d] for d in range(NA_ROWS)])
    return t.transpose(0, 1, 3, 2, 4).reshape(NA_ROWS, H, GRID_W, NA_ROWS * GRID_W).astype(F32)


NA_ROWS_PER_STEP = 4


def _na_kernel(q_ref, k_ref, v_ref, bias_ref, o_ref, *, R):
    nk = NA_ROWS * GRID_W
    for rr in range(NA_ROWS_PER_STEP):
        r = pl.program_id(1) * NA_ROWS_PER_STEP + rr
        r0 = jnp.clip(r - NA_ROWS // 2, 0, R - NA_ROWS)
        delta = r - r0
        ks = pl.multiple_of(r0 * GRID_W, GRID_W)
        rows = slice(rr * GRID_W, (rr + 1) * GRID_W)
        for c in range(BRANCH_W // LANES):
            cols = slice(c * LANES, (c + 1) * LANES)
            qs = _split_heads(q_ref[rows, cols].astype(F32) * HEAD_DIM ** -0.5)
            s = _dot_nt(qs, k_ref[pl.ds(ks, nk), cols])
            s = s + jnp.concatenate([bias_ref[delta, 2 * c], bias_ref[delta, 2 * c + 1]], axis=0)
            m = jnp.max(s, axis=-1, keepdims=True)
            p = jnp.exp(s - m)
            l = jnp.sum(p, axis=-1, keepdims=True)
            o = jnp.dot(p.astype(BF16), v_ref[pl.ds(ks, nk), cols], preferred_element_type=F32) / l
            o_ref[rows, cols] = _merge_heads(o, GRID_W).astype(o_ref.dtype)


def _na_attention(h, bias, B, T):
    R = T // GRID_W
    assert R >= NA_ROWS and R % NA_ROWS_PER_STEP == 0
    M = h.shape[0]
    nr = R // NA_ROWS_PER_STEP
    qrows = NA_ROWS_PER_STEP * GRID_W
    return pl.pallas_call(
        functools.partial(_na_kernel, R=R), name="na_attn",
        grid=(B, nr),
        in_specs=[pl.BlockSpec((qrows, BRANCH_W), lambda b, r: (b * nr + r, COL_QA // BRANCH_W)),
                  pl.BlockSpec((T, BRANCH_W), lambda b, r: (b, COL_KA // BRANCH_W)),
                  pl.BlockSpec((T, BRANCH_W), lambda b, r: (b, COL_VA // BRANCH_W)),
                  pl.BlockSpec(bias.shape, lambda b, r: (0, 0, 0, 0))],
        out_specs=pl.BlockSpec((qrows, BRANCH_W), lambda b, r: (b * nr + r, 0)),
        out_shape=jax.ShapeDtypeStruct((M, BRANCH_W), BF16),
        compiler_params=_params(2),
    )(h, h, h, bias)


DIFF_TQ, DIFF_TK = 512, 512
DIFF_D_LO = -(DIFF_TK + DIFF_TQ)
DIFF_D_HI = 2 * DIFF_TQ
assert DIFF_TQ >= T5_MAX_DIST and DIFF_TK % DIFF_TQ == 0


def _diff_bias_tiles(t5_tab):
    tq, tk = DIFF_TQ, DIFF_TK
    lo = DIFF_D_LO - (tq - 1)
    rel = jnp.arange(lo, DIFF_D_HI + tk, dtype=jnp.int32)
    by_rel = t5_tab[_t5_bucket(rel)].T
    offs = range(DIFF_D_LO, DIFF_D_HI + 1, tq)
    x = jnp.stack([by_rel[:, d - (tq - 1) - lo:d + tk - lo] for d in offs], axis=1)
    return (_toeplitz(x, tq, tk) * LOG2E).astype(F32)


def _diff_kernel(lam_ref, q_ref, k_ref, v_ref, tiles_ref, g_ref, o_ref, *scratch, n_chunks, out_scale):
    qi = pl.program_id(2)
    tq, tk = DIFF_TQ, DIFF_TK
    lane = _lane()
    qf = q_ref[...].astype(F32) * (DIFF_D ** -0.5 * LOG2E)
    qs = jnp.concatenate([(qf * ((lane >> 5) == v).astype(F32)).astype(BF16) for v in range(4)], axis=0)

    def bias_fn(kc, c):
        d = jnp.clip(kc * tk - qi * tq, DIFF_D_LO, DIFF_D_HI)
        j = (d - DIFF_D_LO) // tq
        b0 = tiles_ref[0, j, :, c * LANES:(c + 1) * LANES]
        b1 = tiles_ref[1, j, :, c * LANES:(c + 1) * LANES]
        return jnp.concatenate([b0, b0, b1, b1], axis=0)

    o = _flash(qs, k_ref, v_ref, scratch, n_chunks, tk, bias_fn)
    lam = lam_ref[0]
    x = jnp.where(lane < HEAD_DIM, o[:tq] - lam * o[tq:2 * tq], o[2 * tq:3 * tq] - lam * o[3 * tq:])
    o_ref[...] = (_rms_heads(x, g_ref[...]) * out_scale).astype(o_ref.dtype)


def _diff_attention(h, tiles, lam, g, lam_init, B, T):
    M = h.shape[0]
    tq, tk = DIFF_TQ, DIFF_TK
    assert T % tk == 0
    nq = T // tq
    qb, kb, vb = COL_QB // LANES, COL_KB // LANES, COL_VB // LANES
    g2 = jnp.tile(g.astype(F32), 2).reshape(1, LANES)
    return pl.pallas_call(
        functools.partial(_diff_kernel, n_chunks=T // tk, out_scale=1.0 - lam_init), name="diff_attn",
        grid=(B, DIFF_HEADS // 2, nq),
        in_specs=[pl.BlockSpec(memory_space=pltpu.SMEM),
                  pl.BlockSpec((tq, LANES), lambda b, hp, qi: (b * nq + qi, qb + hp)),
                  pl.BlockSpec((T, LANES), lambda b, hp, qi: (b, kb + hp)),
                  pl.BlockSpec((T, LANES), lambda b, hp, qi: (b, vb + hp)),
                  pl.BlockSpec((2,) + tiles.shape[1:], lambda b, hp, qi: (hp, 0, 0, 0)),
                  pl.BlockSpec((1, LANES), lambda b, hp, qi: (0, 0))],
        out_specs=pl.BlockSpec((tq, LANES), lambda b, hp, qi: (b * nq + qi, hp)),
        out_shape=jax.ShapeDtypeStruct((M, BRANCH_W), BF16),
        scratch_shapes=_flash_scratch(4 * tq, tk),
        compiler_params=_params(3),
    )(lam.reshape(1).astype(F32), h, h, h, tiles, g2)


GQA_TQ, GQA_TK = 256, 512


def _rope_tables(T):
    t = jnp.arange(T, dtype=jnp.int32)
    row = (t // GRID_W).astype(jnp.float32)
    col = (t % GRID_W).astype(jnp.float32)
    freqs = ROPE_THETA ** (-jnp.arange(0, ROPE_AXIS_DIM, 2, dtype=jnp.float32) / ROPE_AXIS_DIM)
    ang = jnp.concatenate([row[:, None] * freqs, col[:, None] * freqs], axis=-1)
    cos, sin = jnp.cos(ang), jnp.sin(ang)
    cos_t = jnp.tile(cos, (1, 4))
    sin_t = jnp.tile(jnp.concatenate([-sin, sin], axis=-1), (1, 2))
    return cos_t.astype(F32), sin_t.astype(F32)


def _rope(x, cos, sin):
    first = (_lane() & (HEAD_DIM - 1)) < HEAD_DIM // 2
    partner = jnp.where(first, pltpu.roll(x, LANES - HEAD_DIM // 2, axis=1), pltpu.roll(x, HEAD_DIM // 2, axis=1))
    return x * cos + partner * sin


def _gqa_kernel(q_ref, k_ref, v_ref, cq_ref, sq_ref, ck_ref, sk_ref, gq_ref, gk_ref, o_ref, kproc_ref, *scratch,
                n_chunks):
    qi = pl.program_id(1)
    tq = GQA_TQ

    @pl.when(qi == 0)
    def _():
        kf = _rms_heads(k_ref[...].astype(F32), gk_ref[...])
        kproc_ref[...] = _rope(kf, ck_ref[...], sk_ref[...]).astype(BF16)

    cq, sq, gq = cq_ref[...], sq_ref[...], gq_ref[...]
    parts = []
    for c in range(BRANCH_W // LANES):
        qf = _rms_heads(q_ref[:, c * LANES:(c + 1) * LANES].astype(F32), gq)
        parts.append(_split_heads(_rope(qf, cq, sq) * (HEAD_DIM ** -0.5 * LOG2E)))
    o = _flash(jnp.concatenate(parts, axis=0), kproc_ref, v_ref, scratch, n_chunks, GQA_TK)
    for c in range(BRANCH_W // LANES):
        o_ref[:, c * LANES:(c + 1) * LANES] = _merge_heads(o[2 * c * tq:(2 * c + 2) * tq], tq).astype(o_ref.dtype)


def _gqa_attention(h, cos_t, sin_t, gq, gk, B, T):
    M = h.shape[0]
    tq, tk = GQA_TQ, GQA_TK
    assert T % tk == 0
    nq = T // tq
    vec = pl.BlockSpec((1, LANES), lambda b, qi: (0, 0))
    qtab = pl.BlockSpec((tq, LANES), lambda b, qi: (qi, 0))
    ktab = pl.BlockSpec((T, LANES), lambda b, qi: (0, 0))
    return pl.pallas_call(
        functools.partial(_gqa_kernel, n_chunks=T // tk), name="gqa_attn",
        grid=(B, nq),
        in_specs=[pl.BlockSpec((tq, BRANCH_W), lambda b, qi: (b * nq + qi, COL_QC // BRANCH_W)),
                  pl.BlockSpec((T, LANES), lambda b, qi: (b, COL_KC // LANES)),
                  pl.BlockSpec((T, LANES), lambda b, qi: (b, COL_VC // LANES)),
                  qtab, qtab, ktab, ktab, vec, vec],
        out_specs=pl.BlockSpec((tq, BRANCH_W), lambda b, qi: (b * nq + qi, 0)),
        out_shape=jax.ShapeDtypeStruct((M, BRANCH_W), BF16),
        scratch_shapes=[pltpu.VMEM((T, LANES), BF16)] + _flash_scratch(GQA_HEADS * tq, tk),
        compiler_params=_params(2),
    )(h, h, h, cos_t, sin_t, cos_t, sin_t, gq, gk)


DIL_TQ = 128
DIL_HALF = 64
assert all(w // (2 * d) == DIL_HALF for w, d in DIL_GROUPS)


def _dil_bias_tiles(t5_tab, g, L):
    dil = DIL_GROUPS[g][1]
    tq = DIL_TQ
    W = min(2 * tq, L)
    offs = range(0, -(W - tq) - 1, -DIL_HALF)
    lo = offs[-1] - (tq - 1)
    rel = jnp.arange(lo, W, dtype=jnp.int32)
    s0 = DIFF_HEADS + g * DIL_SLOTS
    by_rel = t5_tab[:, s0:s0 + DIL_SLOTS][_t5_bucket(rel * dil)]
    by_rel = jnp.where((jnp.abs(rel) <= DIL_HALF)[:, None], by_rel, NEG_INF).T
    x = jnp.stack([by_rel[:, d - (tq - 1) - lo:d + W - lo] for d in offs])
    return _toeplitz(x, tq, W).astype(F32)


DIL_TILES_PER_STEP = 4


def _dil_kernel(q_ref, k_ref, v_ref, tiles_ref, o_ref, lse_ref, *, L, W):
    tq = DIL_TQ
    for tt in range(q_ref.shape[0] // tq):
        n = pl.program_id(2) * (q_ref.shape[0] // tq) + tt
        ws = jnp.clip(n * tq - DIL_HALF, 0, L - W)
        var = (n * tq - ws) // DIL_HALF
        ws = pl.multiple_of(ws, DIL_HALF)
        rows = slice(tt * tq, (tt + 1) * tq)
        for c in range(BRANCH_W // LANES):
            cols = slice(c * LANES, (c + 1) * LANES)
            qs = _split_heads(q_ref[rows, cols].astype(F32) * HEAD_DIM ** -0.5)
            s = _dot_nt(qs, k_ref[pl.ds(ws, W), cols])
            s = s + jnp.concatenate([tiles_ref[var, 2 * c], tiles_ref[var, 2 * c + 1]], axis=0)
            m = jnp.max(s, axis=-1, keepdims=True)
            p = jnp.exp(s - m)
            l = jnp.sum(p, axis=-1, keepdims=True)
            o = jnp.dot(p.astype(BF16), v_ref[pl.ds(ws, W), cols], preferred_element_type=F32) / l
            lse = jnp.broadcast_to(m + jnp.log(l), (2 * tq, LANES))
            o_ref[rows, cols] = _merge_heads(o, tq).astype(o_ref.dtype)
            lse_ref[rows, cols] = _merge_heads(lse, tq)


def _dil_attention(hd, col0, tiles, B, T):
    dil, Md, _ = hd.shape
    L = T // dil
    tq = DIL_TQ
    W = min(2 * tq, L)
    assert L % tq == 0
    tps = _pick(L // tq, DIL_TILES_PER_STEP)
    nq = L // (tq * tps)
    out_spec = pl.BlockSpec((None, tq * tps, BRANCH_W), lambda b, rho, n: (rho, b * nq + n, 0))
    return pl.pallas_call(
        functools.partial(_dil_kernel, L=L, W=W), name=f"dil_attn_{dil}",
        grid=(B, dil, nq),
        in_specs=[pl.BlockSpec((None, tq * tps, BRANCH_W), lambda b, rho, n: (rho, b * nq + n, col0)),
                  pl.BlockSpec((None, L, BRANCH_W), lambda b, rho, n: (rho, b, col0 + 1)),
                  pl.BlockSpec((None, L, BRANCH_W), lambda b, rho, n: (rho, b, col0 + 2)),
                  pl.BlockSpec(tiles.shape, lambda b, rho, n: (0, 0, 0, 0))],
        out_specs=[out_spec, out_spec],
        out_shape=[jax.ShapeDtypeStruct((dil, Md, BRANCH_W), BF16), jax.ShapeDtypeStruct((dil, Md, BRANCH_W), F32)],
        compiler_params=_params(3),
    )(hd, hd, hd, tiles)


def _dil_merge_kernel(o0_ref, o1_ref, o2_ref, l0_ref, l1_ref, l2_ref, o_ref, *scratch):
    def natural(ref, scr):
        dil, rows, _ = ref.shape
        for c in range(scr.shape[0]):
            for rho in range(dil):
                scr[c, pl.ds(rho, rows, stride=dil), :] = ref[rho, :, c * LANES:(c + 1) * LANES].astype(F32)
        return jnp.concatenate([scr[c] for c in range(scr.shape[0])], axis=1)

    os_ = [o0_ref[0].astype(F32), natural(o1_ref, scratch[0]), natural(o2_ref, scratch[1])]
    ls = [l0_ref[0], natural(l1_ref, scratch[2]), natural(l2_ref, scratch[3])]
    mx = jnp.maximum(jnp.maximum(ls[0], ls[1]), ls[2])
    es = [jnp.exp(l - mx) for l in ls]
    den = es[0] + es[1] + es[2]
    num = es[0] * os_[0] + es[1] * os_[1] + es[2] * os_[2]
    o_ref[...] = (num / den).astype(o_ref.dtype)


def _dil_merge(outs, lses):
    M = outs[0].shape[1]
    tm = _pick(M, 512)

    def spec(a):
        dil = a.shape[0]
        return pl.BlockSpec((dil, tm // dil, BRANCH_W), lambda i: (0, i, 0))

    return pl.pallas_call(
        _dil_merge_kernel, name="dil_merge",
        grid=(M // tm,),
        in_specs=[spec(a) for a in (*outs, *lses)],
        out_specs=pl.BlockSpec((tm, BRANCH_W), lambda i: (i, 0)),
        out_shape=jax.ShapeDtypeStruct((M, BRANCH_W), BF16),
        scratch_shapes=[pltpu.VMEM((BRANCH_W // LANES, tm, LANES), F32)] * 4,
        compiler_params=_params(1),
    )(*outs, *lses)


def _deinterleave(w, n_heads):
    d = w.shape[0]
    return w.reshape(d, n_heads, HEAD_DIM // 2, 2).transpose(0, 1, 3, 2).reshape(d, n_heads * HEAD_DIM)


def _prep_layer(l, w_in, w_gate_down, w_gate_up, w_branch, w_out, w_ffn_in, w_ffn_out, ff_pad):
    d = w_in.shape[1]
    wi = w_in[l]
    qc = _deinterleave(wi[:, 3072:3584], GQA_HEADS).reshape(d, GQA_HEADS, HEAD_DIM)
    qc = qc[:, jnp.array(GQA_HEAD_ORDER)].reshape(d, GQA_HEADS * HEAD_DIM)
    kc = _deinterleave(wi[:, 3584:3712], GQA_KV_HEADS)
    qd, kd, vd = wi[:, 3840:5376], wi[:, 5376:6912], wi[:, 6912:8448]

    def group(g):
        s = slice(g * BRANCH_W, (g + 1) * BRANCH_W)
        return [qd[:, s], kd[:, s], vd[:, s]]

    w_proj = jnp.concatenate([wi[:, :3072], qc, kc, wi[:, 3712:3840], w_gate_down[l]]
                             + group(0) + group(1) + group(2), axis=1)
    assert w_proj.shape[1] == H_MAIN_COLS + 6 * BRANCH_W
    wup = w_gate_up[l].reshape(GATE_RANK, N_BRANCH, d).transpose(1, 0, 2)
    wbr = w_branch[l]
    wbr_c = wbr[2].reshape(GQA_HEADS, HEAD_DIM, d)[jnp.array(GQA_HEAD_ORDER)].reshape(BRANCH_W, d)
    wbr = jnp.stack([wbr[0], wbr[1], wbr_c, wbr[3]])
    ff = w_ffn_out.shape[1]
    wfi = w_ffn_in[l]
    pad = ff_pad - ff
    wfi = jnp.concatenate([jnp.pad(wfi[:, :ff], ((0, 0), (0, pad))), jnp.pad(wfi[:, ff:], ((0, 0), (0, pad)))], axis=1)
    wfo = jnp.pad(w_ffn_out[l], ((0, pad), (0, 0)))
    return dict(w_proj=w_proj.astype(BF16), wup=wup.astype(BF16), wbr=wbr.astype(BF16),
                w_out=w_out[l].astype(BF16), wfi=wfi.astype(BF16), wfo=wfo.astype(BF16))


def _trunk(x, consts, layers, p):
    B, T, D = x.shape
    M = B * T
    depth = len(layers)
    alpha = (2 * depth) ** 0.25
    cos_t, sin_t = _rope_tables(T)
    dil_tiles = [_dil_bias_tiles(p['t5_table'], g, T // dil) for g, (_, dil) in enumerate(DIL_GROUPS)]
    xf, xb = _ln_in(x.reshape(M, D), p['ln_in_g'], p['ln_in_b'])
    for l, lw in enumerate(layers):
        lam_init = 0.8 - 0.6 * math.exp(-0.3 * l)
        h3, hd1, hd2 = _projections(xb, lw['w_proj'], 1024, 512)
        h = h3[0]
        o_a = _na_attention(h, consts['na_bias'][l], B, T)
        lq = p['diff_lambda'][l].astype(F32)
        lam = jnp.exp(jnp.sum(lq[0] * lq[1])) - jnp.exp(jnp.sum(lq[2] * lq[3])) + lam_init
        o_b = _diff_attention(h, consts['diff_tiles'], lam, p['diff_subln_g'][l], lam_init, B, T)
        gq = jnp.tile(jnp.concatenate([p['qk_norm_g'][l, 0, 0::2], p['qk_norm_g'][l, 0, 1::2]]), 2).reshape(1, LANES)
        gk = jnp.tile(jnp.concatenate([p['qk_norm_g'][l, 1, 0::2], p['qk_norm_g'][l, 1, 1::2]]), 2).reshape(1, LANES)
        o_c = _gqa_attention(h, cos_t, sin_t, gq.astype(F32), gk.astype(F32), B, T)
        dil_out = [_dil_attention(h3, COL_D0 // BRANCH_W, dil_tiles[0], B, T),
                   _dil_attention(hd1, 0, dil_tiles[1], B, T),
                   _dil_attention(hd2, 0, dil_tiles[2], B, T)]
        o_d = _dil_merge([o for o, _ in dil_out], [s for _, s in dil_out])
        merged = _gate_merge(h, o_a, o_b, o_c, o_d, lw['wup'], p['b_gate'][l].astype(F32), lw['wbr'], 1024, 512)
        xf, xb = _mm_res_ln(merged, lw['w_out'], xf, p['ln1_g'][l], p['ln1_b'][l], alpha, 512, 512, "out_proj_ln")
        ff = _swiglu(xb, lw['wfi'], 1024, 512)
        xf, xb = _mm_res_ln(ff, lw['wfo'], xf, p['ln2_g'][l], p['ln2_b'][l], alpha, 512, 512, "ffn_out_ln")
    return xf.reshape(B, T, D)


def kernel(x_prompt, x_sample, ln_in_g, ln_in_b, w_in, na_rpb, qk_norm_g, diff_lambda, diff_subln_g, t5_table,
           w_gate_down, w_gate_up, b_gate, w_branch, w_out, ln1_g, ln1_b, w_ffn_in, w_ffn_out, ln2_g, ln2_b):
    depth = w_in.shape[0]
    ff = w_ffn_out.shape[1]
    ff_pad = -(-ff // 1024) * 1024
    layers = [_prep_layer(l, w_in, w_gate_down, w_gate_up, w_branch, w_out, w_ffn_in, w_ffn_out, ff_pad)
              for l in range(depth)]
    consts = dict(na_bias=[_na_bias_table(na_rpb[l]) for l in range(depth)],
                  diff_tiles=_diff_bias_tiles(t5_table[:, :DIFF_HEADS]))
    p = dict(ln_in_g=ln_in_g, ln_in_b=ln_in_b, qk_norm_g=qk_norm_g, diff_lambda=diff_lambda,
             diff_subln_g=diff_subln_g, t5_table=t5_table, b_gate=b_gate, ln1_g=ln1_g, ln1_b=ln1_b,
             ln2_g=ln2_g, ln2_b=ln2_b)
    return (_trunk(x_prompt, consts, layers, p), _trunk(x_sample, consts, layers, p))
```

```python
import functools
import math

import jax
import jax.numpy as jnp
from jax import lax
from jax.experimental import pallas as pl
from jax.experimental.pallas import tpu as pltpu

F32 = jnp.float32
BF16 = jnp.bfloat16

HEAD_DIM = 64
LANES = 128
GRID_W = 64
NA_ROWS = 8
NA_COLS = 16
DIFF_HEADS = 8
DIFF_D = HEAD_DIM // 2
GQA_HEADS = 8
GQA_KV_HEADS = 2
ROPE_AXIS_DIM = HEAD_DIM // 2
ROPE_THETA = 10000.0
DIL_SLOTS = 8
DIL_GROUPS = ((128, 1), (512, 4), (2048, 16))
T5_BUCKETS = 32
T5_MAX_DIST = 128
N_BRANCH = 4
BRANCH_W = 8 * HEAD_DIM
GATE_RANK = 256
LN_EPS = 1e-5
RMS_EPS = 1e-6
NEG_INF = -1e30

COL_QA, COL_KA, COL_VA = 0, 512, 1024
COL_QB, COL_KB, COL_VB = 1536, 2048, 2560
COL_QC = 3072
COL_KC, COL_VC = 3584, 3712
COL_GZ = 3840
COL_D0 = 4096
H_MAIN_COLS = COL_D0 + 3 * BRANCH_W
GQA_HEAD_ORDER = (0, 4, 1, 5, 2, 6, 3, 7)

VMEM_LIMIT = 56 * 1024 * 1024


def _params(n_axes, vmem=VMEM_LIMIT):
    return pltpu.CompilerParams(dimension_semantics=("arbitrary",) * n_axes, vmem_limit_bytes=vmem)


def _pick(n, pref):
    t = min(n, pref)
    while n % t:
        t //= 2
    return t


def _ln_rows(y, g, b):
    mu = jnp.mean(y, axis=-1, keepdims=True)
    yc = y - mu
    var = jnp.mean(yc * yc, axis=-1, keepdims=True)
    return yc * lax.rsqrt(var + LN_EPS) * g + b


def _ln_in_kernel(x_ref, g_ref, b_ref, xo_ref, xb_ref):
    r = _ln_rows(x_ref[...], g_ref[...], b_ref[...])
    xo_ref[...] = r
    xb_ref[...] = r.astype(BF16)


def _ln_in(x, g, b):
    M, D = x.shape
    tm = _pick(M, 256)
    row = pl.BlockSpec((tm, D), lambda i: (i, 0))
    vec = pl.BlockSpec((1, D), lambda i: (0, 0))
    return pl.pallas_call(
        _ln_in_kernel, name="ln_in",
        grid=(M // tm,),
        in_specs=[row, vec, vec],
        out_specs=[row, row],
        out_shape=[jax.ShapeDtypeStruct((M, D), F32), jax.ShapeDtypeStruct((M, D), BF16)],
        compiler_params=_params(1),
    )(x, g.reshape(1, D), b.reshape(1, D))


def _proj_kernel(x_ref, w_ref, om_ref, o1_ref, o2_ref, r_ref, *, n_main, n_dil):
    j = pl.program_id(1)
    r = jnp.dot(x_ref[...], w_ref[...], preferred_element_type=F32)

    def residue_major(o_ref):
        dil, rows, _ = o_ref.shape
        for c in range(r_ref.shape[0]):
            cols = slice(c * LANES, (c + 1) * LANES)
            r_ref[c] = r[:, cols]
            for rho in range(dil):
                o_ref[rho, :, cols] = r_ref[c, pl.ds(rho, rows, stride=dil), :].astype(o_ref.dtype)

    @pl.when(j < n_main)
    def _():
        om_ref[0] = r.astype(om_ref.dtype)

    @pl.when((j >= n_main) & (j < n_main + n_dil))
    def _():
        residue_major(o1_ref)

    @pl.when(j >= n_main + n_dil)
    def _():
        residue_major(o2_ref)


def _projections(x, w, tm, tn):
    M, K = x.shape
    tm, tn = _pick(M, tm), _pick(3 * BRANCH_W, tn)
    n_main, n_dil = H_MAIN_COLS // tn, 3 * BRANCH_W // tn
    d1, d2 = DIL_GROUPS[1][1], DIL_GROUPS[2][1]
    assert w.shape[1] == (n_main + 2 * n_dil) * tn and tm % d2 == 0

    def out_spec(dil, first, count):
        return pl.BlockSpec((dil, tm // dil, tn), lambda i, j: (0, i, jnp.clip(j - first, 0, count - 1)))

    return pl.pallas_call(
        functools.partial(_proj_kernel, n_main=n_main, n_dil=n_dil), name="projections",
        grid=(M // tm, n_main + 2 * n_dil),
        in_specs=[pl.BlockSpec((tm, K), lambda i, j: (i, 0)), pl.BlockSpec((K, tn), lambda i, j: (0, j))],
        out_specs=[out_spec(1, 0, n_main), out_spec(d1, n_main, n_dil), out_spec(d2, n_main + n_dil, n_dil)],
        out_shape=[jax.ShapeDtypeStruct((1, M, H_MAIN_COLS), BF16),
                   jax.ShapeDtypeStruct((d1, M // d1, 3 * BRANCH_W), BF16),
                   jax.ShapeDtypeStruct((d2, M // d2, 3 * BRANCH_W), BF16)],
        scratch_shapes=[pltpu.VMEM((tn // LANES, tm, LANES), F32)],
        compiler_params=_params(2),
    )(x, w)


def _swiglu_kernel(x_ref, wa_ref, wb_ref, o_ref):
    x = x_ref[...]
    a = jnp.dot(x, wa_ref[...], preferred_element_type=F32)
    b = jnp.dot(x, wb_ref[...], preferred_element_type=F32)
    o_ref[...] = (a * (1.0 / (1.0 + jnp.exp(-a))) * b).astype(o_ref.dtype)


def _swiglu(x, w, tm, tn):
    M, K = x.shape
    F = w.shape[1] // 2
    tm, tn = _pick(M, tm), _pick(F, tn)
    nf = F // tn
    return pl.pallas_call(
        _swiglu_kernel, name="swiglu",
        grid=(M // tm, nf),
        in_specs=[pl.BlockSpec((tm, K), lambda i, j: (i, 0)),
                  pl.BlockSpec((K, tn), lambda i, j: (0, j)),
                  pl.BlockSpec((K, tn), lambda i, j: (0, j + nf))],
        out_specs=pl.BlockSpec((tm, tn), lambda i, j: (i, j)),
        out_shape=jax.ShapeDtypeStruct((M, F), BF16),
        compiler_params=_params(2),
    )(x, w, w)


MM_LN_COL_CHUNK = 1024
MM_LN_ROW_CHUNK = 128
MM_LN_UNROLL = 1
MM_LN_COL_PIECE = 256


def _mm_res_ln_kernel(a_ref, w_ref, xres_ref, g_ref, b_ref, xo_ref, xb_ref, *, nk, alpha):
    k = pl.program_id(1)
    tm, D = xo_ref.shape

    cw = _pick(D, MM_LN_COL_CHUNK)

    def partial_products(accumulate):
        a = a_ref[...]
        for c in range(D // cw):
            cols = slice(c * cw, (c + 1) * cw)
            part = jnp.dot(a, w_ref[:, cols], preferred_element_type=F32)
            xo_ref[:, cols] = xo_ref[:, cols] + part if accumulate else part

    @pl.when(k == 0)
    def _():
        partial_products(False)

    @pl.when(k > 0)
    def _():
        partial_products(True)

    @pl.when(k == nk - 1)
    def _():
        rw = _pick(tm, MM_LN_ROW_CHUNK)

        pw = _pick(D, MM_LN_COL_PIECE)
        pieces = [slice(c * pw, (c + 1) * pw) for c in range(D // pw)]

        def lane_partials(v):
            return functools.reduce(jnp.add, [v[:, j * LANES:(j + 1) * LANES] for j in range(pw // LANES)])

        def ln_chunk(r, carry):
            rows = pl.ds(pl.multiple_of(r * rw, rw), rw)
            s1 = jnp.zeros((rw, LANES), F32)
            for cs in pieces:
                y = alpha * xres_ref[rows, cs] + xo_ref[rows, cs]
                xo_ref[rows, cs] = y
                s1 = s1 + lane_partials(y)
            mu = jnp.sum(s1, axis=-1, keepdims=True) * (1.0 / D)
            s2 = jnp.zeros((rw, LANES), F32)
            for cs in pieces:
                yc = xo_ref[rows, cs] - mu
                xo_ref[rows, cs] = yc
                s2 = s2 + lane_partials(yc * yc)
            scale = lax.rsqrt(jnp.sum(s2, axis=-1, keepdims=True) * (1.0 / D) + LN_EPS)
            for cs in pieces:
                res = xo_ref[rows, cs] * scale * g_ref[:, cs] + b_ref[:, cs]
                xo_ref[rows, cs] = res
                xb_ref[rows, cs] = res.astype(BF16)
            return carry

        lax.fori_loop(0, tm // rw, ln_chunk, 0, unroll=MM_LN_UNROLL)


def _mm_res_ln(a, w, xres, g, b, alpha, tm, tk, name):
    M, K = a.shape
    D = w.shape[1]
    tm, tk = _pick(M, tm), _pick(K, tk)
    nk = K // tk
    row = pl.BlockSpec((tm, D), lambda i, k: (i, 0))
    vec = pl.BlockSpec((1, D), lambda i, k: (0, 0))
    return pl.pallas_call(
        functools.partial(_mm_res_ln_kernel, nk=nk, alpha=alpha), name=name,
        grid=(M // tm, nk),
        in_specs=[pl.BlockSpec((tm, tk), lambda i, k: (i, k)),
                  pl.BlockSpec((tk, D), lambda i, k: (k, 0)),
                  row, vec, vec],
        out_specs=[row, row],
        out_shape=[jax.ShapeDtypeStruct((M, D), F32), jax.ShapeDtypeStruct((M, D), BF16)],
        compiler_params=_params(2),
    )(a, w, xres, g.reshape(1, D), b.reshape(1, D))


def _gate_merge_kernel(gz_ref, oa_ref, ob_ref, oc_ref, od_ref, wup_ref, bg_ref, wbr_ref, o_ref):
    gz = gz_ref[...]
    acc = None
    for n, o_n in enumerate((oa_ref, ob_ref, oc_ref, od_ref)):
        z = jnp.dot(gz, wup_ref[n], preferred_element_type=F32) + bg_ref[n:n + 1, :]
        t = jnp.dot(o_n[...], wbr_ref[n], preferred_element_type=F32)
        term = t * (1.0 / (1.0 + jnp.exp(-z)))
        acc = term if acc is None else acc + term
    o_ref[...] = acc.astype(o_ref.dtype)


def _gate_merge(h, o_a, o_b, o_c, o_d, wup, bg, wbr, tm, tn):
    M = h.shape[0]
    D = wup.shape[2]
    tm, tn = _pick(M, tm), _pick(D, tn)
    o_spec = pl.BlockSpec((tm, BRANCH_W), lambda i, j: (i, 0))
    return pl.pallas_call(
        _gate_merge_kernel, name="gate_merge",
        grid=(M // tm, D // tn),
        in_specs=[pl.BlockSpec((tm, GATE_RANK), lambda i, j: (i, COL_GZ // GATE_RANK)),
                  o_spec, o_spec, o_spec, o_spec,
                  pl.BlockSpec((N_BRANCH, GATE_RANK, tn), lambda i, j: (0, 0, j)),
                  pl.BlockSpec((N_BRANCH, tn), lambda i, j: (0, j)),
                  pl.BlockSpec((N_BRANCH, BRANCH_W, tn), lambda i, j: (0, 0, j))],
        out_specs=pl.BlockSpec((tm, tn), lambda i, j: (i, j)),
        out_shape=jax.ShapeDtypeStruct((M, D), BF16),
        compiler_params=_params(2),
    )(h, o_a, o_b, o_c, o_d, wup, bg, wbr)


def _lane():
    return lax.broadcasted_iota(jnp.int32, (1, LANES), 1)


def _dot_nt(a, b):
    return lax.dot_general(a, b, (((1,), (1,)), ((), ())), preferred_element_type=F32)


def _split_heads(x):
    lo = (_lane() < HEAD_DIM).astype(F32)
    return jnp.concatenate([(x * lo).astype(BF16), (x * (1.0 - lo)).astype(BF16)], axis=0)


def _merge_heads(o, rows):
    return jnp.where(_lane() < HEAD_DIM, o[:rows], o[rows:])


def _group_mean(x):
    r = lax.broadcasted_iota(jnp.int32, (LANES, LANES), 0) >> 6
    c = lax.broadcasted_iota(jnp.int32, (LANES, LANES), 1) >> 6
    p = jnp.where(r == c, 1.0 / HEAD_DIM, 0.0).astype(BF16)
    hi = x.astype(BF16)
    lo = (x - hi.astype(F32)).astype(BF16)
    return jnp.dot(hi, p, preferred_element_type=F32) + jnp.dot(lo, p, preferred_element_type=F32)


def _rms_heads(x, g):
    return x * lax.rsqrt(_group_mean(x * x) + RMS_EPS) * g


FLASH_ROWS = 16
LOG2E = math.log2(math.e)


def _flash_scratch(rows, tk):
    return [pltpu.VMEM((rows, tk), F32), pltpu.VMEM((rows, tk), BF16)] + [pltpu.VMEM((rows, LANES), F32)] * 4


def _flash(qs, k_ref, v_ref, scratch, n_chunks, tk, bias_fn=None):
    s_ref, p_ref, m_ref, a_ref, l_ref, acc_ref = scratch
    rows = qs.shape[0]
    rb = FLASH_ROWS
    ncol = tk // LANES
    assert rows % rb == 0
    m_ref[...] = jnp.full(m_ref.shape, NEG_INF, F32)
    l_ref[...] = jnp.zeros(l_ref.shape, F32)
    acc_ref[...] = jnp.zeros(acc_ref.shape, F32)

    def chunk(t, carry):
        ks = pl.multiple_of(t * tk, tk)
        s = _dot_nt(qs, k_ref[pl.ds(ks, tk), :])
        mx = None
        for c in range(ncol):
            sc = s[:, c * LANES:(c + 1) * LANES]
            if bias_fn is not None:
                sc = sc + bias_fn(t, c)
            s_ref[:, c * LANES:(c + 1) * LANES] = sc
            mx = sc if mx is None else jnp.maximum(mx, sc)
        m_old = m_ref[...]
        m_new = jnp.maximum(m_old, jnp.broadcast_to(jnp.max(mx, axis=-1, keepdims=True), (rows, LANES)))
        m_ref[...] = m_new
        a_ref[...] = jnp.exp2(m_old - m_new)
        for i in range(rows // rb):
            rs = slice(i * rb, (i + 1) * rb)
            m_blk = m_ref[rs, :]
            p = [jnp.exp2(s_ref[rs, c * LANES:(c + 1) * LANES] - m_blk) for c in range(ncol)]
            l_ref[rs, :] = a_ref[rs, :] * l_ref[rs, :] + functools.reduce(jnp.add, p)
            for c in range(ncol):
                p_ref[rs, c * LANES:(c + 1) * LANES] = p[c].astype(BF16)
        acc_ref[...] = a_ref[...] * acc_ref[...] + jnp.dot(p_ref[...], v_ref[pl.ds(ks, tk), :],
                                                          preferred_element_type=F32)
        return carry

    lax.fori_loop(0, n_chunks, chunk, 0)
    return acc_ref[...] / jnp.sum(l_ref[...], axis=-1, keepdims=True)


def _t5_bucket(rel):
    half = T5_BUCKETS // 2
    max_exact = half // 2
    n = jnp.abs(rel)
    nf = jnp.maximum(n, 1).astype(jnp.float32)
    large = max_exact + (jnp.log(nf / max_exact) / math.log(T5_MAX_DIST / max_exact)
                         * (half - max_exact)).astype(jnp.int32)
    large = jnp.minimum(large, half - 1)
    return jnp.where(rel > 0, half, 0) + jnp.where(n < max_exact, n, large)


def _toeplitz(x, rows, cols):
    n = rows + cols
    pad = n - x.shape[-1]
    x = x[..., :n] if pad <= 0 else jnp.pad(x, [(0, 0)] * (x.ndim - 1) + [(0, pad)])
    t = jnp.tile(x, (1,) * (x.ndim - 1) + (rows,))[..., :rows * (n - 1)]
    return t.reshape(x.shape[:-1] + (rows, n - 1))[..., rows - 1:rows - 1 + cols]


def _na_bias_table(rpb):
    H = rpb.shape[0]
    qcol = jnp.arange(GRID_W, dtype=jnp.int32)
    kcol = jnp.arange(GRID_W, dtype=jnp.int32)
    c0 = jnp.clip(qcol - NA_COLS // 2, 0, GRID_W - NA_COLS)
    col_ok = (kcol[None, :] >= c0[:, None]) & (kcol[None, :] < c0[:, None] + NA_COLS)
    rel = jnp.arange(-(GRID_W - 1), GRID_W, dtype=jnp.int32)
    by_rel = rpb[:, :, jnp.clip(rel + NA_COLS - 1, 0, 2 * NA_COLS - 2)]
    rpb_col = jnp.where(col_ok, _toeplitz(by_rel, GRID_W, GRID_W), NEG_INF)
    t = jnp.stack([rpb_col[:, NA_ROWS - 1 - d:2 * NA_ROWS - 1 - d] for d in range(NA_ROWS)])
    return t.transpose(0, 1, 3, 2, 4).reshape(NA_ROWS, H, GRID_W, NA_ROWS * GRID_W).astype(F32)


NA_ROWS_PER_STEP = 4


def _na_kernel(q_ref, k_ref, v_ref, bias_ref, o_ref, *, R):
    nk = NA_ROWS * GRID_W
    for rr in range(NA_ROWS_PER_STEP):
        r = pl.program_id(1) * NA_ROWS_PER_STEP + rr
        r0 = jnp.clip(r - NA_ROWS // 2, 0, R - NA_ROWS)
        delta = r - r0
        ks = pl.multiple_of(r0 * GRID_W, GRID_W)
        rows = slice(rr * GRID_W, (rr + 1) * GRID_W)
        for c in range(BRANCH_W // LANES):
            cols = slice(c * LANES, (c + 1) * LANES)
            qs = _split_heads(q_ref[rows, cols].astype(F32) * HEAD_DIM ** -0.5)
            s = _dot_nt(qs, k_ref[pl.ds(ks, nk), cols])
            s = s + jnp.concatenate([bias_ref[delta, 2 * c], bias_ref[delta, 2 * c + 1]], axis=0)
            m = jnp.max(s, axis=-1, keepdims=True)
            p = jnp.exp(s - m)
            l = jnp.sum(p, axis=-1, keepdims=True)
            o = jnp.dot(p.astype(BF16), v_ref[pl.ds(ks, nk), cols], preferred_element_type=F32) / l
            o_ref[rows, cols] = _merge_heads(o, GRID_W).astype(o_ref.dtype)


def _na_attention(h, bias, B, T):
    R = T // GRID_W
    assert R >= NA_ROWS and R % NA_ROWS_PER_STEP == 0
    M = h.shape[0]
    nr = R // NA_ROWS_PER_STEP
    qrows = NA_ROWS_PER_STEP * GRID_W
    return pl.pallas_call(
        functools.partial(_na_kernel, R=R), name="na_attn",
        grid=(B, nr),
        in_specs=[pl.BlockSpec((qrows, BRANCH_W), lambda b, r: (b * nr + r, COL_QA // BRANCH_W)),
                  pl.BlockSpec((T, BRANCH_W), lambda b, r: (b, COL_KA // BRANCH_W)),
                  pl.BlockSpec((T, BRANCH_W), lambda b, r: (b, COL_VA // BRANCH_W)),
                  pl.BlockSpec(bias.shape, lambda b, r: (0, 0, 0, 0))],
        out_specs=pl.BlockSpec((qrows, BRANCH_W), lambda b, r: (b * nr + r, 0)),
        out_shape=jax.ShapeDtypeStruct((M, BRANCH_W), BF16),
        compiler_params=_params(2),
    )(h, h, h, bias)


DIFF_TQ, DIFF_TK = 512, 512
DIFF_D_LO = -(DIFF_TK + DIFF_TQ)
DIFF_D_HI = 2 * DIFF_TQ
assert DIFF_TQ >= T5_MAX_DIST and DIFF_TK % DIFF_TQ == 0


def _diff_bias_tiles(t5_tab):
    tq, tk = DIFF_TQ, DIFF_TK
    lo = DIFF_D_LO - (tq - 1)
    rel = jnp.arange(lo, DIFF_D_HI + tk, dtype=jnp.int32)
    by_rel = t5_tab[_t5_bucket(rel)].T
    offs = range(DIFF_D_LO, DIFF_D_HI + 1, tq)
    x = jnp.stack([by_rel[:, d - (tq - 1) - lo:d + tk - lo] for d in offs], axis=1)
    return (_toeplitz(x, tq, tk) * LOG2E).astype(F32)


def _diff_kernel(lam_ref, q_ref, k_ref, v_ref, tiles_ref, g_ref, o_ref, *scratch, n_chunks, out_scale):
    qi = pl.program_id(2)
    tq, tk = DIFF_TQ, DIFF_TK
    lane = _lane()
    qf = q_ref[...].astype(F32) * (DIFF_D ** -0.5 * LOG2E)
    qs = jnp.concatenate([(qf * ((lane >> 5) == v).astype(F32)).astype(BF16) for v in range(4)], axis=0)

    def bias_fn(kc, c):
        d = jnp.clip(kc * tk - qi * tq, DIFF_D_LO, DIFF_D_HI)
        j = (d - DIFF_D_LO) // tq
        b0 = tiles_ref[0, j, :, c * LANES:(c + 1) * LANES]
        b1 = tiles_ref[1, j, :, c * LANES:(c + 1) * LANES]
        return jnp.concatenate([b0, b0, b1, b1], axis=0)

    o = _flash(qs, k_ref, v_ref, scratch, n_chunks, tk, bias_fn)
    lam = lam_ref[0]
    x = jnp.where(lane < HEAD_DIM, o[:tq] - lam * o[tq:2 * tq], o[2 * tq:3 * tq] - lam * o[3 * tq:])
    o_ref[...] = (_rms_heads(x, g_ref[...]) * out_scale).astype(o_ref.dtype)


def _diff_attention(h, tiles, lam, g, lam_init, B, T):
    M = h.shape[0]
    tq, tk = DIFF_TQ, DIFF_TK
    assert T % tk == 0
    nq = T // tq
    qb, kb, vb = COL_QB // LANES, COL_KB // LANES, COL_VB // LANES
    g2 = jnp.tile(g.astype(F32), 2).reshape(1, LANES)
    return pl.pallas_call(
        functools.partial(_diff_kernel, n_chunks=T // tk, out_scale=1.0 - lam_init), name="diff_attn",
        grid=(B, DIFF_HEADS // 2, nq),
        in_specs=[pl.BlockSpec(memory_space=pltpu.SMEM),
                  pl.BlockSpec((tq, LANES), lambda b, hp, qi: (b * nq + qi, qb + hp)),
                  pl.BlockSpec((T, LANES), lambda b, hp, qi: (b, kb + hp)),
                  pl.BlockSpec((T, LANES), lambda b, hp, qi: (b, vb + hp)),
                  pl.BlockSpec((2,) + tiles.shape[1:], lambda b, hp, qi: (hp, 0, 0, 0)),
                  pl.BlockSpec((1, LANES), lambda b, hp, qi: (0, 0))],
        out_specs=pl.BlockSpec((tq, LANES), lambda b, hp, qi: (b * nq + qi, hp)),
        out_shape=jax.ShapeDtypeStruct((M, BRANCH_W), BF16),
        scratch_shapes=_flash_scratch(4 * tq, tk),
        compiler_params=_params(3),
    )(lam.reshape(1).astype(F32), h, h, h, tiles, g2)


GQA_TQ, GQA_TK = 256, 512


def _rope_tables(T):
    t = jnp.arange(T, dtype=jnp.int32)
    row = (t // GRID_W).astype(jnp.float32)
    col = (t % GRID_W).astype(jnp.float32)
    freqs = ROPE_THETA ** (-jnp.arange(0, ROPE_AXIS_DIM, 2, dtype=jnp.float32) / ROPE_AXIS_DIM)
    ang = jnp.concatenate([row[:, None] * freqs, col[:, None] * freqs], axis=-1)
    cos, sin = jnp.cos(ang), jnp.sin(ang)
    cos_t = jnp.tile(cos, (1, 4))
    sin_t = jnp.tile(jnp.concatenate([-sin, sin], axis=-1), (1, 2))
    return cos_t.astype(F32), sin_t.astype(F32)


def _rope(x, cos, sin):
    first = (_lane() & (HEAD_DIM - 1)) < HEAD_DIM // 2
    partner = jnp.where(first, pltpu.roll(x, LANES - HEAD_DIM // 2, axis=1), pltpu.roll(x, HEAD_DIM // 2, axis=1))
    return x * cos + partner * sin


def _gqa_kernel(q_ref, k_ref, v_ref, cq_ref, sq_ref, ck_ref, sk_ref, gq_ref, gk_ref, o_ref, kproc_ref, *scratch,
                n_chunks):
    qi = pl.program_id(1)
    tq = GQA_TQ

    @pl.when(qi == 0)
    def _():
        kf = _rms_heads(k_ref[...].astype(F32), gk_ref[...])
        kproc_ref[...] = _rope(kf, ck_ref[...], sk_ref[...]).astype(BF16)

    cq, sq, gq = cq_ref[...], sq_ref[...], gq_ref[...]
    parts = []
    for c in range(BRANCH_W // LANES):
        qf = _rms_heads(q_ref[:, c * LANES:(c + 1) * LANES].astype(F32), gq)
        parts.append(_split_heads(_rope(qf, cq, sq) * (HEAD_DIM ** -0.5 * LOG2E)))
    o = _flash(jnp.concatenate(parts, axis=0), kproc_ref, v_ref, scratch, n_chunks, GQA_TK)
    for c in range(BRANCH_W // LANES):
        o_ref[:, c * LANES:(c + 1) * LANES] = _merge_heads(o[2 * c * tq:(2 * c + 2) * tq], tq).astype(o_ref.dtype)


def _gqa_attention(h, cos_t, sin_t, gq, gk, B, T):
    M = h.shape[0]
    tq, tk = GQA_TQ, GQA_TK
    assert T % tk == 0
    nq = T // tq
    vec = pl.BlockSpec((1, LANES), lambda b, qi: (0, 0))
    qtab = pl.BlockSpec((tq, LANES), lambda b, qi: (qi, 0))
    ktab = pl.BlockSpec((T, LANES), lambda b, qi: (0, 0))
    return pl.pallas_call(
        functools.partial(_gqa_kernel, n_chunks=T // tk), name="gqa_attn",
        grid=(B, nq),
        in_specs=[pl.BlockSpec((tq, BRANCH_W), lambda b, qi: (b * nq + qi, COL_QC // BRANCH_W)),
                  pl.BlockSpec((T, LANES), lambda b, qi: (b, COL_KC // LANES)),
                  pl.BlockSpec((T, LANES), lambda b, qi: (b, COL_VC // LANES)),
                  qtab, qtab, ktab, ktab, vec, vec],
        out_specs=pl.BlockSpec((tq, BRANCH_W), lambda b, qi: (b * nq + qi, 0)),
        out_shape=jax.ShapeDtypeStruct((M, BRANCH_W), BF16),
        scratch_shapes=[pltpu.VMEM((T, LANES), BF16)] + _flash_scratch(GQA_HEADS * tq, tk),
        compiler_params=_params(2),
    )(h, h, h, cos_t, sin_t, cos_t, sin_t, gq, gk)


DIL_TQ = 128
DIL_HALF = 64
assert all(w // (2 * d) == DIL_HALF for w, d in DIL_GROUPS)


def _dil_bias_tiles(t5_tab, g, L):
    dil = DIL_GROUPS[g][1]
    tq = DIL_TQ
    W = min(2 * tq, L)
    offs = range(0, -(W - tq) - 1, -DIL_HALF)
    lo = offs[-1] - (tq - 1)
    rel = jnp.arange(lo, W, dtype=jnp.int32)
    s0 = DIFF_HEADS + g * DIL_SLOTS
    by_rel = t5_tab[:, s0:s0 + DIL_SLOTS][_t5_bucket(rel * dil)]
    by_rel = jnp.where((jnp.abs(rel) <= DIL_HALF)[:, None], by_rel, NEG_INF).T
    x = jnp.stack([by_rel[:, d - (tq - 1) - lo:d + W - lo] for d in offs])
    return _toeplitz(x, tq, W).astype(F32)


DIL_TILES_PER_STEP = 4


def _dil_kernel(q_ref, k_ref, v_ref, tiles_ref, o_ref, lse_ref, *, L, W):
    tq = DIL_TQ
    for tt in range(q_ref.shape[0] // tq):
        n = pl.program_id(2) * (q_ref.shape[0] // tq) + tt
        ws = jnp.clip(n * tq - DIL_HALF, 0, L - W)
        var = (n * tq - ws) // DIL_HALF
        ws = pl.multiple_of(ws, DIL_HALF)
        rows = slice(tt * tq, (tt + 1) * tq)
        for c in range(BRANCH_W // LANES):
            cols = slice(c * LANES, (c + 1) * LANES)
            qs = _split_heads(q_ref[rows, cols].astype(F32) * HEAD_DIM ** -0.5)
            s = _dot_nt(qs, k_ref[pl.ds(ws, W), cols])
            s = s + jnp.concatenate([tiles_ref[var, 2 * c], tiles_ref[var, 2 * c + 1]], axis=0)
            m = jnp.max(s, axis=-1, keepdims=True)
            p = jnp.exp(s - m)
            l = jnp.sum(p, axis=-1, keepdims=True)
            o = jnp.dot(p.astype(BF16), v_ref[pl.ds(ws, W), cols], preferred_element_type=F32) / l
            lse = jnp.broadcast_to(m + jnp.log(l), (2 * tq, LANES))
            o_ref[rows, cols] = _merge_heads(o, tq).astype(o_ref.dtype)
            lse_ref[rows, cols] = _merge_heads(lse, tq)


def _dil_attention(hd, col0, tiles, B, T):
    dil, Md, _ = hd.shape
    L = T // dil
    tq = DIL_TQ
    W = min(2 * tq, L)
    assert L % tq == 0
    tps = _pick(L // tq, DIL_TILES_PER_STEP)
    nq = L // (tq * tps)
    out_spec = pl.BlockSpec((None, tq * tps, BRANCH_W), lambda b, rho, n: (rho, b * nq + n, 0))
    return pl.pallas_call(
        functools.partial(_dil_kernel, L=L, W=W), name=f"dil_attn_{dil}",
        grid=(B, dil, nq),
        in_specs=[pl.BlockSpec((None, tq * tps, BRANCH_W), lambda b, rho, n: (rho, b * nq + n, col0)),
                  pl.BlockSpec((None, L, BRANCH_W), lambda b, rho, n: (rho, b, col0 + 1)),
                  pl.BlockSpec((None, L, BRANCH_W), lambda b, rho, n: (rho, b, col0 + 2)),
                  pl.BlockSpec(tiles.shape, lambda b, rho, n: (0, 0, 0, 0))],
        out_specs=[out_spec, out_spec],
        out_shape=[jax.ShapeDtypeStruct((dil, Md, BRANCH_W), BF16), jax.ShapeDtypeStruct((dil, Md, BRANCH_W), F32)],
        compiler_params=_params(3),
    )(hd, hd, hd, tiles)


def _dil_merge_kernel(o0_ref, o1_ref, o2_ref, l0_ref, l1_ref, l2_ref, o_ref, *scratch):
    def natural(ref, scr):
        dil, rows, _ = ref.shape
        for c in range(scr.shape[0]):
            for rho in range(dil):
                scr[c, pl.ds(rho, rows, stride=dil), :] = ref[rho, :, c * LANES:(c + 1) * LANES].astype(F32)
        return jnp.concatenate([scr[c] for c in range(scr.shape[0])], axis=1)

    os_ = [o0_ref[0].astype(F32), natural(o1_ref, scratch[0]), natural(o2_ref, scratch[1])]
    ls = [l0_ref[0], natural(l1_ref, scratch[2]), natural(l2_ref, scratch[3])]
    mx = jnp.maximum(jnp.maximum(ls[0], ls[1]), ls[2])
    es = [jnp.exp(l - mx) for l in ls]
    den = es[0] + es[1] + es[2]
    num = es[0] * os_[0] + es[1] * os_[1] + es[2] * os_[2]
    o_ref[...] = (num / den).astype(o_ref.dtype)


def _dil_merge(outs, lses):
    M = outs[0].shape[1]
    tm = _pick(M, 512)

    def spec(a):
        dil = a.shape[0]
        return pl.BlockSpec((dil, tm // dil, BRANCH_W), lambda i: (0, i, 0))

    return pl.pallas_call(
        _dil_merge_kernel, name="dil_merge",
        grid=(M // tm,),
        in_specs=[spec(a) for a in (*outs, *lses)],
        out_specs=pl.BlockSpec((tm, BRANCH_W), lambda i: (i, 0)),
        out_shape=jax.ShapeDtypeStruct((M, BRANCH_W), BF16),
        scratch_shapes=[pltpu.VMEM((BRANCH_W // LANES, tm, LANES), F32)] * 4,
        compiler_params=_params(1),
    )(*outs, *lses)


def _deinterleave(w, n_heads):
    d = w.shape[0]
    return w.reshape(d, n_heads, HEAD_DIM // 2, 2).transpose(0, 1, 3, 2).reshape(d, n_heads * HEAD_DIM)


def _prep_layer(l, w_in, w_gate_down, w_gate_up, w_branch, w_out, w_ffn_in, w_ffn_out, ff_pad):
    d = w_in.shape[1]
    wi = w_in[l]
    qc = _deinterleave(wi[:, 3072:3584], GQA_HEADS).reshape(d, GQA_HEADS, HEAD_DIM)
    qc = qc[:, jnp.array(GQA_HEAD_ORDER)].reshape(d, GQA_HEADS * HEAD_DIM)
    kc = _deinterleave(wi[:, 3584:3712], GQA_KV_HEADS)
    qd, kd, vd = wi[:, 3840:5376], wi[:, 5376:6912], wi[:, 6912:8448]

    def group(g):
        s = slice(g * BRANCH_W, (g + 1) * BRANCH_W)
        return [qd[:, s], kd[:, s], vd[:, s]]

    w_proj = jnp.concatenate([wi[:, :3072], qc, kc, wi[:, 3712:3840], w_gate_down[l]]
                             + group(0) + group(1) + group(2), axis=1)
    assert w_proj.shape[1] == H_MAIN_COLS + 6 * BRANCH_W
    wup = w_gate_up[l].reshape(GATE_RANK, N_BRANCH, d).transpose(1, 0, 2)
    wbr = w_branch[l]
    wbr_c = wbr[2].reshape(GQA_HEADS, HEAD_DIM, d)[jnp.array(GQA_HEAD_ORDER)].reshape(BRANCH_W, d)
    wbr = jnp.stack([wbr[0], wbr[1], wbr_c, wbr[3]])
    ff = w_ffn_out.shape[1]
    wfi = w_ffn_in[l]
    pad = ff_pad - ff
    wfi = jnp.concatenate([jnp.pad(wfi[:, :ff], ((0, 0), (0, pad))), jnp.pad(wfi[:, ff:], ((0, 0), (0, pad)))], axis=1)
    wfo = jnp.pad(w_ffn_out[l], ((0, pad), (0, 0)))
    return dict(w_proj=w_proj.astype(BF16), wup=wup.astype(BF16), wbr=wbr.astype(BF16),
                w_out=w_out[l].astype(BF16), wfi=wfi.astype(BF16), wfo=wfo.astype(BF16))


def _trunk(x, consts, layers, p):
    B, T, D = x.shape
    M = B * T
    depth = len(layers)
    alpha = (2 * depth) ** 0.25
    cos_t, sin_t = _rope_tables(T)
    dil_tiles = [_dil_bias_tiles(p['t5_table'], g, T // dil) for g, (_, dil) in enumerate(DIL_GROUPS)]
    xf, xb = _ln_in(x.reshape(M, D), p['ln_in_g'], p['ln_in_b'])
    for l, lw in enumerate(layers):
        lam_init = 0.8 - 0.6 * math.exp(-0.3 * l)
        h3, hd1, hd2 = _projections(xb, lw['w_proj'], 1024, 512)
        h = h3[0]
        o_a = _na_attention(h, consts['na_bias'][l], B, T)
        lq = p['diff_lambda'][l].astype(F32)
        lam = jnp.exp(jnp.sum(lq[0] * lq[1])) - jnp.exp(jnp.sum(lq[2] * lq[3])) + lam_init
        o_b = _diff_attention(h, consts['diff_tiles'], lam, p['diff_subln_g'][l], lam_init, B, T)
        gq = jnp.tile(jnp.concatenate([p['qk_norm_g'][l, 0, 0::2], p['qk_norm_g'][l, 0, 1::2]]), 2).reshape(1, LANES)
        gk = jnp.tile(jnp.concatenate([p['qk_norm_g'][l, 1, 0::2], p['qk_norm_g'][l, 1, 1::2]]), 2).reshape(1, LANES)
        o_c = _gqa_attention(h, cos_t, sin_t, gq.astype(F32), gk.astype(F32), B, T)
        dil_out = [_dil_attention(h3, COL_D0 // BRANCH_W, dil_tiles[0], B, T),
                   _dil_attention(hd1, 0, dil_tiles[1], B, T),
                   _dil_attention(hd2, 0, dil_tiles[2], B, T)]
        o_d = _dil_merge([o for o, _ in dil_out], [s for _, s in dil_out])
        merged = _gate_merge(h, o_a, o_b, o_c, o_d, lw['wup'], p['b_gate'][l].astype(F32), lw['wbr'], 1024, 512)
        xf, xb = _mm_res_ln(merged, lw['w_out'], xf, p['ln1_g'][l], p['ln1_b'][l], alpha, 512, 512, "out_proj_ln")
        ff = _swiglu(xb, lw['wfi'], 1024, 512)
        xf, xb = _mm_res_ln(ff, lw['wfo'], xf, p['ln2_g'][l], p['ln2_b'][l], alpha, 512, 512, "ffn_out_ln")
    return xf.reshape(B, T, D)


def kernel(x_prompt, x_sample, ln_in_g, ln_in_b, w_in, na_rpb, qk_norm_g, diff_lambda, diff_subln_g, t5_table,
           w_gate_down, w_gate_up, b_gate, w_branch, w_out, ln1_g, ln1_b, w_ffn_in, w_ffn_out, ln2_g, ln2_b):
    depth = w_in.shape[0]
    ff = w_ffn_out.shape[1]
    ff_pad = -(-ff // 1024) * 1024
    layers = [_prep_layer(l, w_in, w_gate_down, w_gate_up, w_branch, w_out, w_ffn_in, w_ffn_out, ff_pad)
              for l in range(depth)]
    consts = dict(na_bias=[_na_bias_table(na_rpb[l]) for l in range(depth)],
                  diff_tiles=_diff_bias_tiles(t5_table[:, :DIFF_HEADS]))
    p = dict(ln_in_g=ln_in_g, ln_in_b=ln_in_b, qk_norm_g=qk_norm_g, diff_lambda=diff_lambda,
             diff_subln_g=diff_subln_g, t5_table=t5_table, b_gate=b_gate, ln1_g=ln1_g, ln1_b=ln1_b,
             ln2_g=ln2_g, ln2_b=ln2_b)
    return (_trunk(x_prompt, consts, layers, p), _trunk(x_sample, consts, layers, p))
```
